```python
import jax
import jax.numpy as jnp
from jax import lax
import numpy as np

D_MODEL = 2048
BATCH = 2
SEQ = 16384
DEPTH = 2

GRID_W = 64
CTX_LEN = 256
NORM_EPS = 1e-6
ROPE_BASE = 10000.0

MLA_HEADS = 8
MLA_Q_RANK = 512
MLA_KV_RANK = 256
MLA_NOPE = 128
MLA_ROPE = 64
MLA_V = 128
MLA_QK = MLA_NOPE + MLA_ROPE
ATTN_BLOCK = 128

NA_HEADS = 8
NA_DIM = 128
NA_WIN_ROWS = 8
NA_WIN_COLS = 16

C_Q_END = MLA_Q_RANK
C_KV_END = C_Q_END + MLA_KV_RANK
K_ROPE_END = C_KV_END + MLA_ROPE
NA_Q_END = K_ROPE_END + NA_HEADS * NA_DIM
NA_K_END = NA_Q_END + NA_HEADS * NA_DIM
ATTN_IN = NA_K_END + NA_HEADS * NA_DIM
ATTN_OUT = MLA_HEADS * MLA_V + NA_HEADS * NA_DIM

POOL_WINDOWS = (2, 4, 8, 16)
POOL_WIDTH = 1024
POOL_GROUP = POOL_WIDTH // len(POOL_WINDOWS)
CONV_CH = 1024
CONV_K = 3
ODD_IN = POOL_WIDTH + 3 * CONV_CH
ODD_OUT = POOL_WIDTH + CONV_CH

N_EXPERTS = 32
TOP_K = 4
D_EXPERT = 2048
SWIGLU_LIMIT = 7.0
SWIGLU_ALPHA = 1.702
MOE_BLOCK = 128

kernel_name = 'hybrid_mla_natten_pool_conv_moe_dit'


def rms_norm(x, gain):
    xf = x.astype(jnp.float32)
    y = xf * lax.rsqrt(jnp.mean(xf * xf, axis=-1, keepdims=True) + NORM_EPS)
    return (y * gain.astype(jnp.float32)).astype(x.dtype)


def modulate(x, gain, shift, scale):
    return rms_norm(x, gain) * (1 + scale) + shift


def split_heads(t, n_heads):
    b, l, _ = t.shape
    return t.reshape(b, l, n_heads, -1).transpose(0, 2, 1, 3)


def merge_heads(t):
    b, h, l, d = t.shape
    return t.transpose(0, 2, 1, 3).reshape(b, l, h * d)


def axial_rope_angles(n_tokens, dim):
    t = jnp.arange(n_tokens, dtype=jnp.int32)
    row = (t // GRID_W).astype(jnp.float32)
    col = (t % GRID_W).astype(jnp.float32)
    n_freq = dim // 4
    inv_freq = ROPE_BASE ** (-jnp.arange(n_freq, dtype=jnp.float32) / n_freq)
    return row[:, None] * inv_freq, col[:, None] * inv_freq


def _rotate(t, ang):
    t1, t2 = jnp.split(t, 2, axis=-1)
    cos, sin = jnp.cos(ang), jnp.sin(ang)
    return jnp.concatenate([t1 * cos - t2 * sin, t1 * sin + t2 * cos], axis=-1)


def rope_tail(t, ang_r, ang_c):
    nope = t[..., :MLA_NOPE]
    rot = t[..., MLA_NOPE:].astype(jnp.float32)
    half = MLA_ROPE // 2
    rot = jnp.concatenate([_rotate(rot[..., :half], ang_r), _rotate(rot[..., half:], ang_c)], axis=-1)
    return jnp.concatenate([nope, rot.astype(t.dtype)], axis=-1)


def mla_queries(c_q, q_norm, w_q_up, qk_q):
    b, l, _ = c_q.shape
    q = (rms_norm(c_q, q_norm) @ w_q_up).reshape(b, l, MLA_HEADS, MLA_QK)
    return rms_norm(q, qk_q).transpose(0, 2, 1, 3)


def mla_keys_values(c_kv, k_rope, kv_norm, w_kv_up, qk_k):
    b, l, _ = c_kv.shape
    kv = (rms_norm(c_kv, kv_norm) @ w_kv_up).reshape(b, l, MLA_HEADS, MLA_NOPE + MLA_V)
    k_nope, v = jnp.split(kv, [MLA_NOPE], axis=-1)
    k_r = jnp.broadcast_to(k_rope[:, :, None, :], (b, l, MLA_HEADS, MLA_ROPE))
    k = rms_norm(jnp.concatenate([k_nope, k_r], axis=-1), qk_k)
    return k.transpose(0, 2, 1, 3), v.transpose(0, 2, 1, 3)


def blocked_attention(q, k, v, scale):
    b, h, l, dq = q.shape
    nb = l // ATTN_BLOCK
    qb = q.reshape(b, h, nb, ATTN_BLOCK, dq).transpose(2, 0, 1, 3, 4)

    def one_block(qi):
        s = jnp.einsum('bhqd,bhkd->bhqk', qi, k, preferred_element_type=jnp.float32) * scale
        p = jax.nn.softmax(s, axis=-1)
        return jnp.einsum('bhqk,bhkd->bhqd', p.astype(v.dtype), v)

    o = lax.map(one_block, qb)
    return o.transpose(1, 2, 0, 3, 4).reshape(b, h, l, v.shape[-1])


def neighbourhood_attention(q, k, v, k_ctx, v_ctx, rel_bias):
    b, h, l, d = q.shape
    rows = l // GRID_W
    win_r = min(NA_WIN_ROWS, rows)
    win_c = NA_WIN_COLS
    n_win = win_r * win_c
    scale = d ** -0.5
    qg = q.reshape(b, h, rows, GRID_W, d)
    kg = k.reshape(b, h, rows, GRID_W, d)
    vg = v.reshape(b, h, rows, GRID_W, d)
    row_start = jnp.clip(jnp.arange(rows) - win_r // 2, 0, rows - win_r)
    cols = jnp.arange(GRID_W)
    col_idx = jnp.clip(cols - win_c // 2, 0, GRID_W - win_c)[:, None] + jnp.arange(win_c)
    col_bias = rel_bias[:, :, col_idx - cols[:, None] + (NA_WIN_COLS - 1)]

    def one_row(r):
        r0 = row_start[r]
        kb = lax.dynamic_slice_in_dim(kg, r0, win_r, axis=2)[:, :, :, col_idx]
        vb = lax.dynamic_slice_in_dim(vg, r0, win_r, axis=2)[:, :, :, col_idx]
        qr = lax.dynamic_index_in_dim(qg, r, axis=2, keepdims=False)
        s_win = jnp.einsum('bhqd,bhrqcd->bhqrc', qr, kb, preferred_element_type=jnp.float32) * scale
        bias = jnp.take(col_bias, r0 + jnp.arange(win_r) - r + (NA_WIN_ROWS - 1), axis=1)
        s_win = s_win + bias.transpose(0, 2, 1, 3)[None].astype(jnp.float32)
        s_ctx = jnp.einsum('bhqd,bhkd->bhqk', qr, k_ctx, preferred_element_type=jnp.float32) * scale
        p = jax.nn.softmax(jnp.concatenate([s_win.reshape(b, h, GRID_W, n_win), s_ctx], axis=-1), axis=-1)
        p_win = p[..., :n_win].reshape(b, h, GRID_W, win_r, win_c).astype(v.dtype)
        p_ctx = p[..., n_win:].astype(v.dtype)
        return (jnp.einsum('bhqrc,bhrqcd->bhqd', p_win, vb)
                + jnp.einsum('bhqk,bhkd->bhqd', p_ctx, v_ctx))

    o = lax.map(one_row, jnp.arange(rows))
    return o.transpose(1, 2, 0, 3, 4).reshape(b, h, l, d)


def attention_mixer(h, h_ctx, w_in, w_out, mla_q_norm, mla_w_q_up, mla_kv_norm, mla_w_kv_up,
                    mla_qk_q, mla_qk_k, na_qk_q, na_qk_k, na_rel_bias):
    l = h.shape[1]
    u = h @ w_in
    u_ctx = h_ctx @ jnp.concatenate([w_in[:, C_Q_END:K_ROPE_END], w_in[:, NA_Q_END:]], axis=1)
    ctx_na_k = MLA_KV_RANK + MLA_ROPE
    ctx_na_v = ctx_na_k + NA_HEADS * NA_DIM
    ang_r, ang_c = axial_rope_angles(l, MLA_ROPE)

    q_m = rope_tail(mla_queries(u[..., :C_Q_END], mla_q_norm, mla_w_q_up, mla_qk_q), ang_r, ang_c)
    k_m, v_m = mla_keys_values(u[..., C_Q_END:C_KV_END], u[..., C_KV_END:K_ROPE_END],
                               mla_kv_norm, mla_w_kv_up, mla_qk_k)
    k_m = rope_tail(k_m, ang_r, ang_c)
    kc_m, vc_m = mla_keys_values(u_ctx[..., :MLA_KV_RANK], u_ctx[..., MLA_KV_RANK:ctx_na_k],
                                 mla_kv_norm, mla_w_kv_up, mla_qk_k)
    o_m = blocked_attention(q_m, jnp.concatenate([k_m, kc_m], axis=2),
                            jnp.concatenate([v_m, vc_m], axis=2), MLA_QK ** -0.5)

    q_n = rms_norm(split_heads(u[..., K_ROPE_END:NA_Q_END], NA_HEADS), na_qk_q)
    k_n = rms_norm(split_heads(u[..., NA_Q_END:NA_K_END], NA_HEADS), na_qk_k)
    v_n = split_heads(u[..., NA_K_END:], NA_HEADS)
    kc_n = rms_norm(split_heads(u_ctx[..., ctx_na_k:ctx_na_v], NA_HEADS), na_qk_k)
    vc_n = split_heads(u_ctx[..., ctx_na_v:], NA_HEADS)
    o_n = neighbourhood_attention(q_n, k_n, v_n, kc_n, vc_n, na_rel_bias)

    return jnp.concatenate([merge_heads(o_m), merge_heads(o_n)], axis=-1) @ w_out


def multiscale_pool(u, pool_w, pool_scale):
    b, l, ch = u.shape
    uf = u.astype(jnp.float32)
    cs = jnp.pad(jnp.cumsum(uf, axis=1), ((0, 0), (1, 0), (0, 0)))
    t = jnp.arange(l)
    means = []
    for g, w in enumerate(POOL_WINDOWS):
        lo = jnp.clip(t - w // 2, 0, l)
        hi = jnp.clip(t - w // 2 + w, 0, l)
        csg = cs[:, :, g * POOL_GROUP:(g + 1) * POOL_GROUP]
        means.append((csg[:, hi] - csg[:, lo]) / (hi - lo).astype(jnp.float32)[None, :, None])
    pooled = (jnp.concatenate(means, axis=-1) - uf).astype(u.dtype)
    mixed = jnp.einsum('blgc,gcd->blgd', pooled.reshape(b, l, len(POOL_WINDOWS), POOL_GROUP), pool_w)
    return mixed.reshape(b, l, ch) * pool_scale


def depthwise_conv3(z, conv_w):
    return lax.conv_general_dilated(
        z, conv_w[:, None, :].astype(z.dtype), window_strides=(1,),
        padding=[(CONV_K // 2, CONV_K // 2)], dimension_numbers=('NWC', 'WIO', 'NWC'),
        feature_group_count=z.shape[-1])


def pool_conv_mixer(h, w_in, w_out, pool_w, pool_scale, conv_w):
    u = h @ w_in
    p_in, gate_b, gate_c, val = jnp.split(u, [POOL_WIDTH, POOL_WIDTH + CONV_CH, POOL_WIDTH + 2 * CONV_CH], axis=-1)
    y_pool = multiscale_pool(p_in, pool_w, pool_scale)
    y_conv = gate_b * depthwise_conv3(gate_c * val, conv_w)
    return jnp.concatenate([y_pool, y_conv], axis=-1) @ w_out


def moe_ffn(h, w_router, b_router, w_gate_up, b_gate_up, w_down, b_down):
    b, l, d = h.shape
    t = h.reshape(-1, d)
    n = t.shape[0]
    logits = (t @ w_router).astype(jnp.float32) + b_router.astype(jnp.float32)
    top_val, top_idx = lax.top_k(logits, TOP_K)
    gates = jax.nn.softmax(top_val, axis=-1)
    flat_e = top_idx.reshape(-1)
    n_slots = n * TOP_K
    order = jnp.argsort(flat_e)
    sorted_e = flat_e[order]
    counts = jnp.bincount(flat_e, length=N_EXPERTS)
    padded = (counts + MOE_BLOCK - 1) // MOE_BLOCK * MOE_BLOCK
    start = jnp.cumsum(counts) - counts
    pend = jnp.cumsum(padded)
    pstart = pend - padded
    dest = pstart[sorted_e] + jnp.arange(n_slots) - start[sorted_e]
    n_blocks = -(-n_slots // MOE_BLOCK) + N_EXPERTS
    n_rows = n_blocks * MOE_BLOCK
    row_tok = jnp.full((n_rows,), n, jnp.int32).at[dest].set((order // TOP_K).astype(jnp.int32))
    row_gate = jnp.zeros((n_rows,), jnp.float32).at[dest].set(gates.reshape(-1)[order])
    block_expert = jnp.minimum(
        jnp.searchsorted(pend, jnp.arange(n_blocks) * MOE_BLOCK, side='right'), N_EXPERTS - 1)
    t_pad = jnp.concatenate([t, jnp.zeros((1, d), t.dtype)], axis=0)

    def expert_block(args):
        e, tok, g = args
        hu = t_pad[tok] @ w_gate_up[e] + b_gate_up[e]
        a_gate, a_up = jnp.split(hu, 2, axis=-1)
        a_gate = jnp.minimum(a_gate, SWIGLU_LIMIT)
        a_up = jnp.clip(a_up, -SWIGLU_LIMIT, SWIGLU_LIMIT)
        act = (a_up + 1) * (a_gate * jax.nn.sigmoid(SWIGLU_ALPHA * a_gate))
        y = act @ w_down[e] + b_down[e]
        return y * g[:, None].astype(y.dtype)

    ys = lax.map(expert_block, (block_expert, row_tok.reshape(n_blocks, MOE_BLOCK),
                                row_gate.reshape(n_blocks, MOE_BLOCK)))
    out = jax.ops.segment_sum(ys.reshape(n_rows, d), row_tok, num_segments=n + 1)[:n]
    return out.reshape(b, l, d)


def setup_inputs(seed: int = 0) -> dict:
    key = jax.random.key(seed)
    keys = iter(jax.random.split(key, 48))
    D = D_MODEL

    def normal(shape, std):
        return std * jax.random.normal(next(keys), shape, jnp.float32)

    def gain(n):
        return 1.0 + 0.05 * jax.random.normal(next(keys), (n,), jnp.float32)

    inp = {}
    inp['x'] = normal((BATCH, SEQ, D), 1.0)
    inp['c'] = normal((BATCH, D), 1.0)
    inp['ctx'] = normal((BATCH, CTX_LEN, D), 1.0)
    inp['c_ctx'] = normal((D,), 1.0)

    def add_moe(p):
        inp[p + 'norm2'] = gain(D)
        inp[p + 'w_router'] = normal((D, N_EXPERTS), D ** -0.5)
        inp[p + 'b_router'] = normal((N_EXPERTS,), 0.01)
        inp[p + 'w_gate_up'] = normal((N_EXPERTS, D, 2 * D_EXPERT), D ** -0.5)
        inp[p + 'b_gate_up'] = normal((N_EXPERTS, 2 * D_EXPERT), 0.02)
        inp[p + 'w_down'] = normal((N_EXPERTS, D_EXPERT, D), D_EXPERT ** -0.5)
        inp[p + 'b_down'] = normal((N_EXPERTS, D), 0.02)

    inp['l0_w_mod'] = normal((D, 6 * D), 0.5 * D ** -0.5)
    inp['l0_b_mod'] = normal((6 * D,), 0.02)
    inp['l0_norm1'] = gain(D)
    inp['l0_w_in'] = normal((D, ATTN_IN), D ** -0.5)
    inp['l0_mla_q_norm'] = gain(MLA_Q_RANK)
    inp['l0_mla_w_q_up'] = normal((MLA_Q_RANK, MLA_HEADS * MLA_QK), MLA_Q_RANK ** -0.5)
    inp['l0_mla_kv_norm'] = gain(MLA_KV_RANK)
    inp['l0_mla_w_kv_up'] = normal((MLA_KV_RANK, MLA_HEADS * (MLA_NOPE + MLA_V)), MLA_KV_RANK ** -0.5)
    inp['l0_mla_qk_q'] = gain(MLA_QK)
    inp['l0_mla_qk_k'] = gain(MLA_QK)
    inp['l0_na_qk_q'] = gain(NA_DIM)
    inp['l0_na_qk_k'] = gain(NA_DIM)
    inp['l0_na_rel_bias'] = normal((NA_HEADS, 2 * NA_WIN_ROWS - 1, 2 * NA_WIN_COLS - 1), 0.1)
    inp['l0_w_out'] = normal((ATTN_OUT, D), ATTN_OUT ** -0.5)
    add_moe('l0_')

    inp['l1_w_mod'] = normal((D, 6 * D), 0.5 * D ** -0.5)
    inp['l1_b_mod'] = normal((6 * D,), 0.02)
    inp['l1_norm1'] = gain(D)
    inp['l1_w_in'] = normal((D, ODD_IN), D ** -0.5)
    inp['l1_pool_w'] = normal((len(POOL_WINDOWS), POOL_GROUP, POOL_GROUP), POOL_GROUP ** -0.5)
    inp['l1_pool_scale'] = gain(POOL_WIDTH)
    inp['l1_conv_w'] = normal((CONV_K, CONV_CH), CONV_K ** -0.5)
    inp['l1_w_out'] = normal((ODD_OUT, D), ODD_OUT ** -0.5)
    add_moe('l1_')
    return inp


def reference(x, c, ctx, c_ctx,
              l0_w_mod, l0_b_mod, l0_norm1, l0_w_in, l0_mla_q_norm, l0_mla_w_q_up, l0_mla_kv_norm,
              l0_mla_w_kv_up, l0_mla_qk_q, l0_mla_qk_k, l0_na_qk_q, l0_na_qk_k, l0_na_rel_bias, l0_w_out,
              l0_norm2, l0_w_router, l0_b_router, l0_w_gate_up, l0_b_gate_up, l0_w_down, l0_b_down,
              l1_w_mod, l1_b_mod, l1_norm1, l1_w_in, l1_pool_w, l1_pool_scale, l1_conv_w, l1_w_out,
              l1_norm2, l1_w_router, l1_b_router, l1_w_gate_up, l1_b_gate_up, l1_w_down, l1_b_down):
    silu_c = jax.nn.silu(c)
    silu_cc = jax.nn.silu(c_ctx)
    layers = [
        dict(w_mod=l0_w_mod, b_mod=l0_b_mod, norm1=l0_norm1, w_in=l0_w_in, w_out=l0_w_out,
             mla_q_norm=l0_mla_q_norm, mla_w_q_up=l0_mla_w_q_up, mla_kv_norm=l0_mla_kv_norm,
             mla_w_kv_up=l0_mla_w_kv_up, mla_qk_q=l0_mla_qk_q, mla_qk_k=l0_mla_qk_k,
             na_qk_q=l0_na_qk_q, na_qk_k=l0_na_qk_k, na_rel_bias=l0_na_rel_bias,
             moe=(l0_w_router, l0_b_router, l0_w_gate_up, l0_b_gate_up, l0_w_down, l0_b_down),
             norm2=l0_norm2),
        dict(w_mod=l1_w_mod, b_mod=l1_b_mod, norm1=l1_norm1, w_in=l1_w_in, w_out=l1_w_out,
             pool_w=l1_pool_w, pool_scale=l1_pool_scale, conv_w=l1_conv_w,
             moe=(l1_w_router, l1_b_router, l1_w_gate_up, l1_b_gate_up, l1_w_down, l1_b_down),
             norm2=l1_norm2),
    ]
    for i in range(DEPTH):
        p = layers[i]
        mod = (silu_c @ p['w_mod'] + p['b_mod'])[:, None, :]
        shift1, scale1, gate1, shift2, scale2, gate2 = jnp.split(mod, 6, axis=-1)
        h = modulate(x, p['norm1'], shift1, scale1)
        if i % 2 == 0:
            mod_c = silu_cc @ p['w_mod'][:, :2 * D_MODEL] + p['b_mod'][:2 * D_MODEL]
            h_ctx = modulate(ctx, p['norm1'], mod_c[:D_MODEL], mod_c[D_MODEL:])
            y = attention_mixer(h, h_ctx, p['w_in'], p['w_out'], p['mla_q_norm'], p['mla_w_q_up'],
                                p['mla_kv_norm'], p['mla_w_kv_up'], p['mla_qk_q'], p['mla_qk_k'],
                                p['na_qk_q'], p['na_qk_k'], p['na_rel_bias'])
        else:
            y = pool_conv_mixer(h, p['w_in'], p['w_out'], p['pool_w'], p['pool_scale'], p['conv_w'])
        x = x + gate1 * y
        x = x + gate2 * moe_ffn(modulate(x, p['norm2'], shift2, scale2), *p['moe'])
    return x
```

```python
import functools

import numpy as np
import jax
import jax.numpy as jnp
from jax import lax
from jax.experimental import pallas as pl
from jax.experimental.pallas import tpu as pltpu

F32 = jnp.float32
BF16 = jnp.bfloat16

V7X_LANES = 128
V7X_SUBLANES = 8
V7X_VMEM_BYTES = 64 * 2**20
VMEM_LIMIT_BYTES = V7X_VMEM_BYTES * 7 // 8

GRID_W = 64
NORM_EPS = 1e-6
ROPE_BASE = 10000.0
MLA_HEADS = 8
MLA_Q_RANK = 512
MLA_KV_RANK = 256
MLA_NOPE = 128
MLA_ROPE = 64
MLA_V = 128
MLA_QK = MLA_NOPE + MLA_ROPE
MLA_QK_PAD = 256
NA_HEADS = 8
NA_DIM = 128
NA_WIN_ROWS = 8
NA_WIN_COLS = 16
NA_Q_ROWS = 8
NA_K_ROWS = 16
POOL_WINDOWS = (2, 4, 8, 16)
POOL_WIDTH = 1024
POOL_GROUP = POOL_WIDTH // len(POOL_WINDOWS)
CONV_CH = 1024
HALO = 16
N_EXPERTS = 32
TOP_K = 4
SWIGLU_LIMIT = 7.0
SWIGLU_ALPHA = 1.702
NEG_BIG = -1e30

U0_COLS = 4096
U0_CKV_BLK = 4
U0_NAQ_BLK = 7
U0_NAK_BLK = 15
U0_NAV_BLK = 23

ROW_TILE = 512
MOE_ROWS = 512
MOE_F_TILE = 512
COMBINE_TOKENS = 256


def _cparams(*sem):
    return pltpu.CompilerParams(dimension_semantics=sem, vmem_limit_bytes=VMEM_LIMIT_BYTES)


def _rms(x, gain):
    ms = jnp.mean(x * x, axis=-1, keepdims=True)
    return x * lax.rsqrt(ms + NORM_EPS) * gain


def _dot_nt(a, b):
    return lax.dot_general(a, b, (((1,), (1,)), ((), ())), preferred_element_type=F32)


def _row_pitch(s):
    p = -(-s // V7X_SUBLANES) * V7X_SUBLANES
    return p + V7X_SUBLANES if p % (2 * V7X_SUBLANES) == 0 else p


def _mod_kernel(c_ref, w_ref, b_ref, o_ref):
    c = c_ref[...]
    s = c * jax.nn.sigmoid(c)
    o_ref[...] = jnp.dot(s, w_ref[...], preferred_element_type=F32,
                         precision=lax.Precision.HIGHEST) + b_ref[...]


def _modulation(c_rows, w_mod, b_mod):
    d, n = w_mod.shape
    tn = 512
    assert n % tn == 0
    return pl.pallas_call(
        _mod_kernel,
        out_shape=jax.ShapeDtypeStruct((V7X_SUBLANES, n), F32),
        grid=(n // tn,),
        in_specs=[pl.BlockSpec((V7X_SUBLANES, d), lambda j: (0, 0)),
                  pl.BlockSpec((d, tn), lambda j: (0, j)),
                  pl.BlockSpec((1, tn), lambda j: (0, j))],
        out_specs=pl.BlockSpec((V7X_SUBLANES, tn), lambda j: (0, j)),
        compiler_params=_cparams("parallel"),
        name="modulation",
    )(c_rows, w_mod, b_mod.reshape(1, n))


def _norm_proj_kernel(x_ref, g_ref, sh_ref, sc_ref, w_ref, o_ref, h_scr):
    @pl.when(pl.program_id(1) == 0)
    def _():
        h = _rms(x_ref[...], g_ref[...]) * (1.0 + sc_ref[...]) + sh_ref[...]
        h_scr[...] = h.astype(BF16)

    o_ref[...] = jnp.dot(h_scr[...], w_ref[...], preferred_element_type=F32).astype(o_ref.dtype)


def _norm_proj(x2, gain, mod3, shift_blk, scale_blk, group_of_tile, w_bf16, tm):
    n, d = x2.shape
    ncol = w_bf16.shape[1]
    tn = 1024
    return pl.pallas_call(
        _norm_proj_kernel,
        out_shape=jax.ShapeDtypeStruct((n, ncol), BF16),
        grid=(n // tm, ncol // tn),
        in_specs=[pl.BlockSpec((tm, d), lambda i, j: (i, 0)),
                  pl.BlockSpec((1, d), lambda i, j: (0, 0)),
                  pl.BlockSpec((None, 1, d), lambda i, j: (group_of_tile(i), 0, shift_blk)),
                  pl.BlockSpec((None, 1, d), lambda i, j: (group_of_tile(i), 0, scale_blk)),
                  pl.BlockSpec((d, tn), lambda i, j: (0, j))],
        out_specs=pl.BlockSpec((tm, tn), lambda i, j: (i, j)),
        scratch_shapes=[pltpu.VMEM((tm, d), BF16)],
        compiler_params=_cparams("parallel", "arbitrary"),
        name="norm_proj",
    )(x2, gain.reshape(1, d), mod3, mod3, w_bf16)


def _lane_iota(shape):
    return lax.broadcasted_iota(jnp.int32, shape, len(shape) - 1)


def _q_prep_kernel(cq_ref, qn_ref, wq_ref, g_ref, cos_ref, sin_ref, o_ref, cn_scr):
    @pl.when(pl.program_id(2) == 0)
    def _():
        cn_scr[...] = _rms(cq_ref[...].astype(F32), qn_ref[...]).astype(BF16)

    t = jnp.dot(cn_scr[...], wq_ref[...], preferred_element_type=F32)
    sq = jnp.where(_lane_iota(t.shape) < MLA_QK, t * t, 0.0)
    r = lax.rsqrt(jnp.sum(sq, axis=-1, keepdims=True) * (1.0 / MLA_QK) + NORM_EPS)
    tn = t * r * g_ref[...]
    rope = tn[:, MLA_NOPE:MLA_QK] * cos_ref[...] + tn[:, MLA_QK:] * sin_ref[...]
    out = jnp.concatenate([tn[:, :MLA_NOPE], rope, jnp.zeros_like(rope)], axis=-1)
    o_ref[...] = (out * (MLA_QK ** -0.5)).astype(o_ref.dtype)


def _q_prep(u, b, l, q_norm, wq, gq, cos_t, sin_t, tm):
    nt = l // tm
    return pl.pallas_call(
        _q_prep_kernel,
        out_shape=jax.ShapeDtypeStruct((b, MLA_HEADS, l, MLA_QK_PAD), BF16),
        grid=(b, nt, MLA_HEADS),
        in_specs=[pl.BlockSpec((tm, MLA_Q_RANK), lambda bi, i, h: (bi * nt + i, 0)),
                  pl.BlockSpec((1, MLA_Q_RANK), lambda bi, i, h: (0, 0)),
                  pl.BlockSpec((None, MLA_Q_RANK, MLA_QK_PAD), lambda bi, i, h: (h, 0, 0)),
                  pl.BlockSpec((1, MLA_QK_PAD), lambda bi, i, h: (0, 0)),
                  pl.BlockSpec((tm, MLA_ROPE), lambda bi, i, h: (i, 0)),
                  pl.BlockSpec((tm, MLA_ROPE), lambda bi, i, h: (i, 0))],
        out_specs=pl.BlockSpec((None, None, tm, MLA_QK_PAD), lambda bi, i, h: (bi, h, i, 0)),
        scratch_shapes=[pltpu.VMEM((tm, MLA_Q_RANK), BF16)],
        compiler_params=_cparams("parallel", "parallel", "arbitrary"),
        name="mla_q_prep",
    )(u, q_norm.reshape(1, -1), wq, gq, cos_t, sin_t)


def _kv_prep_kernel(ckv_ref, kvn_ref, wkv_ref, g_ref, cos_ref, sin_ref, k_ref, v_ref, cn_scr):
    @pl.when(pl.program_id(2) == 0)
    def _():
        cn_scr[...] = _rms(ckv_ref[:, :MLA_KV_RANK].astype(F32), kvn_ref[...]).astype(BF16)

    t = jnp.dot(cn_scr[...], wkv_ref[...], preferred_element_type=F32)
    kn = t[:, :MLA_NOPE]
    kr = ckv_ref[:, MLA_KV_RANK:].astype(F32)
    ss = (jnp.sum(kn * kn, axis=-1, keepdims=True)
          + jnp.sum(jnp.where(_lane_iota(kr.shape) < MLA_ROPE, kr * kr, 0.0), axis=-1, keepdims=True))
    r = lax.rsqrt(ss * (1.0 / MLA_QK) + NORM_EPS)
    g = g_ref[...]
    kn_n = kn * r * g[:, :MLA_NOPE]
    kr_n = kr * r * g[:, MLA_NOPE:]
    rope = kr_n[:, :MLA_ROPE] * cos_ref[...] + kr_n[:, MLA_ROPE:] * sin_ref[...]
    k_ref[...] = jnp.concatenate([kn_n, rope, jnp.zeros_like(rope)], axis=-1).astype(k_ref.dtype)
    v_ref[...] = t[:, MLA_NOPE:].astype(v_ref.dtype)


def _kv_prep(ckv, kv_norm, wkv, gk, cos_t, sin_t, tm):
    b, m, w = ckv.shape
    return pl.pallas_call(
        _kv_prep_kernel,
        out_shape=(jax.ShapeDtypeStruct((b, MLA_HEADS, m, MLA_QK_PAD), BF16),
                   jax.ShapeDtypeStruct((b, MLA_HEADS, m, MLA_V), BF16)),
        grid=(b, m // tm, MLA_HEADS),
        in_specs=[pl.BlockSpec((None, tm, w), lambda bi, i, h: (bi, i, 0)),
                  pl.BlockSpec((1, MLA_KV_RANK), lambda bi, i, h: (0, 0)),
                  pl.BlockSpec((None, MLA_KV_RANK, MLA_NOPE + MLA_V), lambda bi, i, h: (h, 0, 0)),
                  pl.BlockSpec((1, MLA_QK_PAD), lambda bi, i, h: (0, 0)),
                  pl.BlockSpec((tm, MLA_ROPE), lambda bi, i, h: (i, 0)),
                  pl.BlockSpec((tm, MLA_ROPE), lambda bi, i, h: (i, 0))],
        out_specs=(pl.BlockSpec((None, None, tm, MLA_QK_PAD), lambda bi, i, h: (bi, h, i, 0)),
                   pl.BlockSpec((None, None, tm, MLA_V), lambda bi, i, h: (bi, h, i, 0))),
        scratch_shapes=[pltpu.VMEM((tm, MLA_KV_RANK), BF16)],
        compiler_params=_cparams("parallel", "parallel", "arbitrary"),
        name="mla_kv_prep",
    )(ckv, kv_norm.reshape(1, -1), wkv, gk, cos_t, sin_t)


def _mla_kernel(q_ref, k_ref, v_ref, o_ref, m_scr, l_scr, acc_scr, *, tk, nk):
    m_scr[...] = jnp.full_like(m_scr, NEG_BIG)
    l_scr[...] = jnp.zeros_like(l_scr)
    acc_scr[...] = jnp.zeros_like(acc_scr)
    q = q_ref[...]

    def body(c, carry):
        off = pl.multiple_of(c * tk, tk)
        s = _dot_nt(q, k_ref[pl.ds(off, tk), :])
        m_prev = m_scr[...]
        m_new = jnp.maximum(m_prev, jnp.max(s, axis=-1, keepdims=True))
        alpha = jnp.exp(m_prev - m_new)
        p = jnp.exp(s - m_new)
        l_scr[...] = alpha * l_scr[...] + jnp.sum(p, axis=-1, keepdims=True)
        acc_scr[...] = alpha * acc_scr[...] + jnp.dot(
            p.astype(BF16), v_ref[pl.ds(off, tk), :], preferred_element_type=F32)
        m_scr[...] = m_new
        return carry

    lax.fori_loop(0, nk, body, 0)
    o_ref[...] = (acc_scr[...] / l_scr[...]).astype(o_ref.dtype)


def _mla_attention(q, k, v, tq):
    b, h, l, _ = q.shape
    m = k.shape[2]
    tk = next(t for t in (1280, 1024, 768, 512, 256, 128) if m % t == 0)
    return pl.pallas_call(
        functools.partial(_mla_kernel, tk=tk, nk=m // tk),
        out_shape=jax.ShapeDtypeStruct((b, l, h * MLA_V), BF16),
        grid=(b, h, l // tq),
        in_specs=[pl.BlockSpec((None, None, tq, MLA_QK_PAD), lambda bi, hi, i: (bi, hi, i, 0)),
                  pl.BlockSpec((None, None, m, MLA_QK_PAD), lambda bi, hi, i: (bi, hi, 0, 0)),
                  pl.BlockSpec((None, None, m, MLA_V), lambda bi, hi, i: (bi, hi, 0, 0))],
        out_specs=pl.BlockSpec((None, tq, MLA_V), lambda bi, hi, i: (bi, i, hi)),
        scratch_shapes=[pltpu.VMEM((tq, 1), F32), pltpu.VMEM((tq, 1), F32),
                        pltpu.VMEM((tq, MLA_V), F32)],
        compiler_params=_cparams("parallel", "parallel", "arbitrary"),
        name="mla_attention",
    )(q, k, v)


def _na_kernel(q_ref, k_ref, v_ref, kc_ref, vc_ref, gq_ref, gk_ref, bias_ref, o_ref, *, rows):
    rb = pl.program_id(2)
    k_row0 = jnp.clip(rb * NA_Q_ROWS - NA_WIN_ROWS // 2, 0, rows - NA_K_ROWS)
    off = pl.multiple_of(k_row0 * GRID_W, NA_WIN_ROWS // 2 * GRID_W)
    nk = NA_K_ROWS * GRID_W
    q = (_rms(q_ref[...].astype(F32), gq_ref[...]) * (NA_DIM ** -0.5)).astype(BF16)
    kw = _rms(k_ref[pl.ds(off, nk), :].astype(F32), gk_ref[...]).astype(BF16)
    kc = _rms(kc_ref[...].astype(F32), gk_ref[...]).astype(BF16)
    s_w = _dot_nt(q, kw) + bias_ref[...]
    s_c = _dot_nt(q, kc)
    m = jnp.maximum(jnp.max(s_w, axis=-1, keepdims=True), jnp.max(s_c, axis=-1, keepdims=True))
    p_w = jnp.exp(s_w - m)
    p_c = jnp.exp(s_c - m)
    denom = jnp.sum(p_w, axis=-1, keepdims=True) + jnp.sum(p_c, axis=-1, keepdims=True)
    o = (jnp.dot(p_w.astype(BF16), v_ref[pl.ds(off, nk), :], preferred_element_type=F32)
         + jnp.dot(p_c.astype(BF16), vc_ref[...], preferred_element_type=F32))
    o_ref[...] = (o / denom).astype(o_ref.dtype)


def _na_bias_table(rel_bias, rows):
    nrb = rows // NA_Q_ROWS
    tables = []
    for rb in (0, min(1, nrb - 1), nrb - 1):
        k_row0 = int(np.clip(rb * NA_Q_ROWS - NA_WIN_ROWS // 2, 0, rows - NA_K_ROWS))
        r = rb * NA_Q_ROWS + np.arange(NA_Q_ROWS)
        kr = k_row0 + np.arange(NA_K_ROWS)
        qc = np.arange(GRID_W)
        kc = np.arange(GRID_W)
        rs = np.clip(r - NA_WIN_ROWS // 2, 0, rows - NA_WIN_ROWS)
        cs = np.clip(qc - NA_WIN_COLS // 2, 0, GRID_W - NA_WIN_COLS)
        row_ok = (kr[None, :] >= rs[:, None]) & (kr[None, :] < rs[:, None] + NA_WIN_ROWS)
        col_ok = (kc[None, :] >= cs[:, None]) & (kc[None, :] < cs[:, None] + NA_WIN_COLS)
        ri = np.clip(kr[None, :] - r[:, None] + NA_WIN_ROWS - 1, 0, 2 * NA_WIN_ROWS - 2)
        ci = np.clip(kc[None, :] - qc[:, None] + NA_WIN_COLS - 1, 0, 2 * NA_WIN_COLS - 2)
        bias = rel_bias[:, ri[:, None, :, None], ci[None, :, None, :]]
        ok = row_ok[:, None, :, None] & col_ok[None, :, None, :]
        bias = jnp.where(ok[None], bias.astype(F32), NEG_BIG)
        tables.append(bias.reshape(NA_HEADS, NA_Q_ROWS * GRID_W, NA_K_ROWS * GRID_W))
    return jnp.stack(tables)


def _na_attention(u3, uctx3, gq, gk, bias):
    b, l, _ = u3.shape
    n_ctx = uctx3.shape[1]
    rows = l // GRID_W
    nrb = rows // NA_Q_ROWS
    tq = NA_Q_ROWS * GRID_W

    def pattern(rb):
        return jnp.where(rb == 0, 0, jnp.where(rb == nrb - 1, 2, 1))

    return pl.pallas_call(
        functools.partial(_na_kernel, rows=rows),
        out_shape=jax.ShapeDtypeStruct((b, l, NA_HEADS * NA_DIM), BF16),
        grid=(b, NA_HEADS, nrb),
        in_specs=[pl.BlockSpec((None, tq, NA_DIM), lambda bi, h, rb: (bi, rb, U0_NAQ_BLK + h)),
                  pl.BlockSpec((None, l, NA_DIM), lambda bi, h, rb: (bi, 0, U0_NAK_BLK + h)),
                  pl.BlockSpec((None, l, NA_DIM), lambda bi, h, rb: (bi, 0, U0_NAV_BLK + h)),
                  pl.BlockSpec((None, n_ctx, NA_DIM), lambda bi, h, rb: (bi, 0, U0_NAK_BLK + h)),
                  pl.BlockSpec((None, n_ctx, NA_DIM), lambda bi, h, rb: (bi, 0, U0_NAV_BLK + h)),
                  pl.BlockSpec((1, NA_DIM), lambda bi, h, rb: (0, 0)),
                  pl.BlockSpec((1, NA_DIM), lambda bi, h, rb: (0, 0)),
                  pl.BlockSpec((None, None, tq, NA_K_ROWS * GRID_W),
                               lambda bi, h, rb: (pattern(rb), h, 0, 0))],
        out_specs=pl.BlockSpec((None, tq, NA_DIM), lambda bi, h, rb: (bi, rb, h)),
        compiler_params=_cparams("parallel", "parallel", "arbitrary"),
        name="neighbourhood_attention",
    )(u3, u3, u3, uctx3, uctx3, gq.reshape(1, -1), gk.reshape(1, -1), bias)


def _attn_out_kernel(am_ref, an_ref, wm_ref, wn_ref, x_ref, gate_ref, o_ref):
    y = (jnp.dot(am_ref[...], wm_ref[...], preferred_element_type=F32)
         + jnp.dot(an_ref[...], wn_ref[...], preferred_element_type=F32))
    o_ref[...] = x_ref[...] + gate_ref[...] * y


def _attn_out(a_m, a_n, w_out_bf16, x2, mod3, gate_blk, tiles_per_batch, tm):
    n, d = x2.shape
    half = a_m.shape[1]
    return pl.pallas_call(
        _attn_out_kernel,
        out_shape=jax.ShapeDtypeStruct((n, d), F32),
        grid=(n // tm,),
        in_specs=[pl.BlockSpec((tm, half), lambda i: (i, 0)),
                  pl.BlockSpec((tm, half), lambda i: (i, 0)),
                  pl.BlockSpec((half, d), lambda i: (0, 0)),
                  pl.BlockSpec((half, d), lambda i: (1, 0)),
                  pl.BlockSpec((tm, d), lambda i: (i, 0)),
                  pl.BlockSpec((None, 1, d), lambda i: (i // tiles_per_batch, 0, gate_blk))],
        out_specs=pl.BlockSpec((tm, d), lambda i: (i, 0)),
        compiler_params=_cparams("parallel"),
        name="attn_out_residual",
    )(a_m, a_n, w_out_bf16, w_out_bf16, x2, mod3)


def _pool_conv_kernel(prev_ref, u_ref, next_ref, pw_ref, ps_ref, cw_ref, wp_ref, wc_ref, x_ref,
                      gate_ref, o_ref, *, seq_len, tiles_per_batch):
    tm = u_ref.shape[0]
    n_ext = tm + 2 * HALO
    pos0 = (pl.program_id(0) % tiles_per_batch) * tm
    pos_ext = pos0 - HALO + lax.broadcasted_iota(jnp.int32, (n_ext, 1), 0)
    valid = (pos_ext >= 0) & (pos_ext < seq_len)

    def ext(c0, c1):
        e = jnp.concatenate([prev_ref[:, c0:c1], u_ref[:, c0:c1], next_ref[:, c0:c1]], axis=0)
        return jnp.where(valid, e.astype(F32), 0.0)

    def shifted(a, d):
        return pltpu.roll(a, (-d) % n_ext, axis=0)

    pos = (pos0 + lax.broadcasted_iota(jnp.int32, (tm, 1), 0)).astype(F32)
    y = jnp.zeros((tm, x_ref.shape[1]), F32)
    for g, w in enumerate(POOL_WINDOWS):
        c0 = g * POOL_GROUP
        e = ext(c0, c0 + POOL_GROUP)
        acc = e + shifted(e, -1)
        span = 1
        while 2 * span < w:
            acc = shifted(acc, span) + shifted(acc, -span)
            span *= 2
        half = w // 2
        cnt = jnp.minimum(pos + half, float(seq_len)) - jnp.maximum(pos - half, 0.0)
        pooled = acc[HALO:HALO + tm] / cnt - e[HALO:HALO + tm]
        mixed = jnp.dot(pooled.astype(BF16), pw_ref[g], preferred_element_type=F32)
        y_g = (mixed * ps_ref[:, c0:c0 + POOL_GROUP]).astype(BF16)
        y = y + jnp.dot(y_g, wp_ref[c0:c0 + POOL_GROUP, :], preferred_element_type=F32)

    gb0 = POOL_WIDTH
    gc0 = POOL_WIDTH + CONV_CH
    v0 = POOL_WIDTH + 2 * CONV_CH
    z = ext(gc0, gc0 + CONV_CH) * ext(v0, v0 + CONV_CH)
    cw = cw_ref[...]
    conv = cw[0:1] * shifted(z, -1) + cw[1:2] * z + cw[2:3] * shifted(z, 1)
    y_conv = u_ref[:, gb0:gb0 + CONV_CH].astype(F32) * conv[HALO:HALO + tm]
    y = y + jnp.dot(y_conv.astype(BF16), wc_ref[...], preferred_element_type=F32)
    o_ref[...] = x_ref[...] + gate_ref[...] * y


def _pool_conv_mixer(u, pool_w, pool_scale, conv_w, w_out_bf16, x2, mod3, gate_blk, seq_len, tm):
    n, d = x2.shape
    width = u.shape[1]
    tiles_per_batch = seq_len // tm
    hb = tm // HALO
    last = n // HALO - 1
    return pl.pallas_call(
        functools.partial(_pool_conv_kernel, seq_len=seq_len, tiles_per_batch=tiles_per_batch),
        out_shape=jax.ShapeDtypeStruct((n, d), F32),
        grid=(n // tm,),
        in_specs=[pl.BlockSpec((HALO, width), lambda i: (jnp.maximum(i * hb - 1, 0), 0)),
                  pl.BlockSpec((tm, width), lambda i: (i, 0)),
                  pl.BlockSpec((HALO, width), lambda i: (jnp.minimum((i + 1) * hb, last), 0)),
                  pl.BlockSpec(pool_w.shape, lambda i: (0, 0, 0)),
                  pl.BlockSpec((1, POOL_WIDTH), lambda i: (0, 0)),
                  pl.BlockSpec((V7X_SUBLANES, CONV_CH), lambda i: (0, 0)),
                  pl.BlockSpec((POOL_WIDTH, d), lambda i: (0, 0)),
                  pl.BlockSpec((CONV_CH, d), lambda i: (1, 0)),
                  pl.BlockSpec((tm, d), lambda i: (i, 0)),
                  pl.BlockSpec((None, 1, d), lambda i: (i // tiles_per_batch, 0, gate_blk))],
        out_specs=pl.BlockSpec((tm, d), lambda i: (i, 0)),
        compiler_params=_cparams("parallel"),
        name="pool_conv_mixer",
    )(u, u, u, pool_w.astype(BF16), pool_scale.reshape(1, -1),
      jnp.pad(conv_w, ((0, V7X_SUBLANES - conv_w.shape[0]), (0, 0))), w_out_bf16, w_out_bf16, x2, mod3)


def _split_bf16(a):
    hi = a.astype(BF16)
    return hi, (a - hi.astype(F32)).astype(BF16)


def _router_kernel(x_ref, g_ref, sh_ref, sc_ref, wr_ref, br_ref, h_ref, idx_ref, gate_ref, *, s_rows):
    tm = x_ref.shape[0]
    h = _rms(x_ref[...], g_ref[...]) * (1.0 + sc_ref[...]) + sh_ref[...]
    for s in range(s_rows):
        h_ref[pl.ds(s, tm, stride=s_rows), :] = h[:, s * V7X_LANES:(s + 1) * V7X_LANES]
    h_hi, h_lo = _split_bf16(h)
    w_hi, w_lo = _split_bf16(wr_ref[...])
    logits = _dot_nt(w_hi, h_hi) + _dot_nt(w_hi, h_lo) + _dot_nt(w_lo, h_hi) + br_ref[...]
    expert = lax.broadcasted_iota(jnp.int32, logits.shape, 0)
    vals, ids = [], []
    for _ in range(TOP_K):
        mx = jnp.max(logits, axis=0, keepdims=True)
        am = jnp.min(jnp.where(logits == mx, expert, N_EXPERTS), axis=0, keepdims=True)
        vals.append(mx)
        ids.append(am)
        logits = jnp.where(expert == am, -jnp.inf, logits)
    ex = [jnp.exp(v - vals[0]) for v in vals]
    tot = ex[0] + ex[1] + ex[2] + ex[3]
    idx_ref[...] = jnp.concatenate(ids, axis=0)
    gate_ref[...] = jnp.concatenate([e / tot for e in ex], axis=0)


def _router(x2, gain, mod3, shift_blk, scale_blk, tiles_per_batch, w_router, b_router, tm):
    n, d = x2.shape
    s_rows = d // V7X_LANES
    return pl.pallas_call(
        functools.partial(_router_kernel, s_rows=s_rows),
        out_shape=(jax.ShapeDtypeStruct((n * s_rows, V7X_LANES), F32),
                   jax.ShapeDtypeStruct((TOP_K, n), jnp.int32),
                   jax.ShapeDtypeStruct((TOP_K, n), F32)),
        grid=(n // tm,),
        in_specs=[pl.BlockSpec((tm, d), lambda i: (i, 0)),
                  pl.BlockSpec((1, d), lambda i: (0, 0)),
                  pl.BlockSpec((None, 1, d), lambda i: (i // tiles_per_batch, 0, shift_blk)),
                  pl.BlockSpec((None, 1, d), lambda i: (i // tiles_per_batch, 0, scale_blk)),
                  pl.BlockSpec((N_EXPERTS, d), lambda i: (0, 0)),
                  pl.BlockSpec((N_EXPERTS, 1), lambda i: (0, 0))],
        out_specs=(pl.BlockSpec((tm * s_rows, V7X_LANES), lambda i: (i, 0)),
                   pl.BlockSpec((TOP_K, tm), lambda i: (0, i)),
                   pl.BlockSpec((TOP_K, tm), lambda i: (0, i))),
        compiler_params=_cparams("parallel"),
        name="moe_router",
    )(x2, gain.reshape(1, d), mod3, mod3, w_router.T, b_router.reshape(-1, 1))


def _gather_rows(idx_hbm_row, src_hbm, idx_smem, buf, sem, idx_sem, n, s_rows, pitch):
    cp = pltpu.make_async_copy(idx_hbm_row, idx_smem, idx_sem)
    cp.start()
    cp.wait()

    def row_copy(r, src_row):
        dst = buf.at[pl.ds(pl.multiple_of(r * pitch, V7X_SUBLANES), s_rows)]
        return pltpu.make_async_copy(src_hbm.at[src_row], dst, sem)

    def issue(r, c):
        row_copy(r, idx_smem[r]).start()
        return c

    def drain(r, c):
        row_copy(r, 0).wait()
        return c

    lax.fori_loop(0, n, issue, 0, unroll=8)
    lax.fori_loop(0, n, drain, 0, unroll=8)


def _experts_kernel(be_ref, nused_ref, tok_hbm, h_hbm, wg_ref, wu_ref, bg_ref, bu_ref, wd_ref,
                    bd_ref, gate_ref, y_ref, idx_smem, gbuf, x_scr, acc_scr, sem, idx_sem,
                    *, s_rows, pitch):
    i = pl.program_id(0)
    j = pl.program_id(1)
    tb = x_scr.shape[0]
    active = i < nused_ref[0]

    @pl.when(active & (j == 0))
    def _():
        _gather_rows(tok_hbm.at[i], h_hbm, idx_smem, gbuf, sem, idx_sem, tb, s_rows, pitch)
        for s in range(s_rows):
            x_scr[:, s * V7X_LANES:(s + 1) * V7X_LANES] = (
                gbuf[pl.ds(s, tb, stride=pitch), :].astype(BF16))
        acc_scr[...] = jnp.zeros_like(acc_scr)

    @pl.when(active)
    def _():
        x = x_scr[...]
        a_gate = jnp.dot(x, wg_ref[...], preferred_element_type=F32) + bg_ref[...]
        a_up = jnp.dot(x, wu_ref[...], preferred_element_type=F32) + bu_ref[...]
        a_gate = jnp.minimum(a_gate, SWIGLU_LIMIT)
        a_up = jnp.clip(a_up, -SWIGLU_LIMIT, SWIGLU_LIMIT)
        act = (a_up + 1.0) * (a_gate * jax.nn.sigmoid(SWIGLU_ALPHA * a_gate))
        acc_scr[...] += jnp.dot(act.astype(BF16), wd_ref[...], preferred_element_type=F32)

    last = j == pl.num_programs(1) - 1

    @pl.when(active & last)
    def _():
        y = (acc_scr[...] + bd_ref[...]) * gate_ref[...]
        for s in range(s_rows):
            y_ref[pl.ds(s, tb, stride=s_rows), :] = y[:, s * V7X_LANES:(s + 1) * V7X_LANES]

    @pl.when(jnp.logical_not(active) & last)
    def _():
        y_ref[...] = jnp.zeros_like(y_ref)


def _experts(block_expert, n_used, row_tok, row_gate, h_rows, w_gate_up, b_gate_up, w_down, b_down):
    n_blocks, tb = row_tok.shape
    e, d, two_f = w_gate_up.shape
    f = two_f // 2
    tf = MOE_F_TILE
    nf = f // tf
    s_rows = d // V7X_LANES
    pitch = _row_pitch(s_rows)

    def fcol(i, j, nused):
        return jnp.where(i < nused[0], j, nf - 1)

    grid_spec = pltpu.PrefetchScalarGridSpec(
        num_scalar_prefetch=2,
        grid=(n_blocks, nf),
        in_specs=[pl.BlockSpec(memory_space=pl.ANY),
                  pl.BlockSpec(memory_space=pl.ANY),
                  pl.BlockSpec((None, d, tf), lambda i, j, be, nu: (be[i], 0, fcol(i, j, nu))),
                  pl.BlockSpec((None, d, tf), lambda i, j, be, nu: (be[i], 0, nf + fcol(i, j, nu))),
                  pl.BlockSpec((None, 1, tf), lambda i, j, be, nu: (be[i], 0, fcol(i, j, nu))),
                  pl.BlockSpec((None, 1, tf), lambda i, j, be, nu: (be[i], 0, nf + fcol(i, j, nu))),
                  pl.BlockSpec((None, tf, d), lambda i, j, be, nu: (be[i], fcol(i, j, nu), 0)),
                  pl.BlockSpec((None, 1, d), lambda i, j, be, nu: (be[i], 0, 0)),
                  pl.BlockSpec((tb, 1), lambda i, j, be, nu: (i, 0))],
        out_specs=pl.BlockSpec((tb * s_rows, V7X_LANES), lambda i, j, be, nu: (i, 0)),
        scratch_shapes=[pltpu.SMEM((tb,), jnp.int32),
                        pltpu.VMEM((tb * pitch, V7X_LANES), F32),
                        pltpu.VMEM((tb, d), BF16),
                        pltpu.VMEM((tb, d), F32),
                        pltpu.SemaphoreType.DMA,
                        pltpu.SemaphoreType.DMA])
    return pl.pallas_call(
        functools.partial(_experts_kernel, s_rows=s_rows, pitch=pitch),
        out_shape=jax.ShapeDtypeStruct((n_blocks * tb * s_rows, V7X_LANES), F32),
        grid_spec=grid_spec,
        compiler_params=_cparams("arbitrary", "arbitrary"),
        name="moe_experts",
    )(block_expert, n_used, row_tok, h_rows, w_gate_up, w_gate_up,
      b_gate_up.reshape(e, 1, two_f), b_gate_up.reshape(e, 1, two_f), w_down,
      b_down.reshape(e, 1, d), row_gate.reshape(-1, 1))


def _combine_kernel(pos_hbm, y_hbm, x_ref, gate_ref, o_ref, idx_smem, gbuf, sem, idx_sem,
                    *, s_rows, pitch):
    tt = x_ref.shape[0]
    _gather_rows(pos_hbm.at[pl.program_id(0)], y_hbm, idx_smem, gbuf, sem, idx_sem,
                 TOP_K * tt, s_rows, pitch)
    for s in range(s_rows):
        tot = gbuf[pl.ds(s, tt, stride=pitch), :]
        for k in range(1, TOP_K):
            tot = tot + gbuf[pl.ds(k * tt * pitch + s, tt, stride=pitch), :]
        cols = slice(s * V7X_LANES, (s + 1) * V7X_LANES)
        o_ref[:, cols] = x_ref[:, cols] + gate_ref[:, cols] * tot


def _combine(pos_tiles, y_rows, x2, mod3, gate_blk, tiles_per_batch, tt):
    n, d = x2.shape
    s_rows = d // V7X_LANES
    pitch = _row_pitch(s_rows)
    return pl.pallas_call(
        functools.partial(_combine_kernel, s_rows=s_rows, pitch=pitch),
        out_shape=jax.ShapeDtypeStruct((n, d), F32),
        grid=(n // tt,),
        in_specs=[pl.BlockSpec(memory_space=pl.ANY),
                  pl.BlockSpec(memory_space=pl.ANY),
                  pl.BlockSpec((tt, d), lambda i: (i, 0)),
                  pl.BlockSpec((None, 1, d), lambda i: (i // tiles_per_batch, 0, gate_blk))],
        out_specs=pl.BlockSpec((tt, d), lambda i: (i, 0)),
        scratch_shapes=[pltpu.SMEM((TOP_K * tt,), jnp.int32),
                        pltpu.VMEM((TOP_K * tt * pitch, V7X_LANES), F32),
                        pltpu.SemaphoreType.DMA,
                        pltpu.SemaphoreType.DMA],
        compiler_params=_cparams("arbitrary"),
        name="moe_combine",
    )(pos_tiles, y_rows, x2, mod3)


def _routing_tables(idx_t, gate_t, tb):
    k, n = idx_t.shape
    n_slots = k * n
    flat_e = idx_t.reshape(-1)
    order = jnp.argsort(flat_e, stable=True).astype(jnp.int32)
    sorted_e = flat_e[order]
    counts = jnp.bincount(flat_e, length=N_EXPERTS).astype(jnp.int32)
    padded = (counts + tb - 1) // tb * tb
    start = jnp.cumsum(counts) - counts
    pend = jnp.cumsum(padded)
    pstart = pend - padded
    dest = pstart[sorted_e] + jnp.arange(n_slots, dtype=jnp.int32) - start[sorted_e]
    n_blocks = -(-n_slots // tb) + N_EXPERTS
    n_rows = n_blocks * tb
    row_tok = jnp.zeros((n_rows,), jnp.int32).at[dest].set(order % n)
    row_gate = jnp.zeros((n_rows,), F32).at[dest].set(gate_t.reshape(-1)[order])
    pos = jnp.zeros((n_slots,), jnp.int32).at[order].set(dest)
    block_expert = jnp.minimum(
        jnp.searchsorted(pend, jnp.arange(n_blocks, dtype=jnp.int32) * tb, side="right"),
        N_EXPERTS - 1).astype(jnp.int32)
    n_used = (pend[-1] // tb).astype(jnp.int32).reshape(1)
    return block_expert, n_used, row_tok.reshape(n_blocks, tb), row_gate, pos.reshape(k, n)


def _moe(x2, norm2, mod3, tiles_per_batch_fn, w_router, b_router, w_gate_up, b_gate_up, w_down,
         b_down):
    n, d = x2.shape
    s_rows = d // V7X_LANES
    h_rows, idx_t, gate_t = _router(x2, norm2, mod3, 3, 4, tiles_per_batch_fn(ROW_TILE),
                                    w_router, b_router, ROW_TILE)
    block_expert, n_used, row_tok, row_gate, pos = _routing_tables(idx_t, gate_t, MOE_ROWS)
    y_rows = _experts(block_expert, n_used, row_tok, row_gate,
                      h_rows.reshape(n, s_rows, V7X_LANES),
                      w_gate_up.astype(BF16), b_gate_up, w_down.astype(BF16), b_down)
    tt = COMBINE_TOKENS
    pos_tiles = pos.reshape(TOP_K, n // tt, tt).transpose(1, 0, 2).reshape(n // tt, TOP_K * tt)
    return _combine(pos_tiles, y_rows.reshape(-1, s_rows, V7X_LANES), x2, mod3, 5,
                    tiles_per_batch_fn(tt), tt)


def _rope_swap_perm():
    half = MLA_ROPE // 2
    quarter = half // 2
    j = np.arange(MLA_ROPE)
    return (j // half) * half + (j % half + quarter) % half


def _rope_tables(n_tokens):
    t = jnp.arange(n_tokens, dtype=jnp.int32)
    row = (t // GRID_W).astype(F32)
    col = (t % GRID_W).astype(F32)
    n_freq = MLA_ROPE // 4
    inv_freq = ROPE_BASE ** (-jnp.arange(n_freq, dtype=F32) / n_freq)
    ar = row[:, None] * inv_freq
    ac = col[:, None] * inv_freq
    cos_t = jnp.concatenate([jnp.cos(ar), jnp.cos(ar), jnp.cos(ac), jnp.cos(ac)], axis=-1)
    sin_t = jnp.concatenate([-jnp.sin(ar), jnp.sin(ar), -jnp.sin(ac), jnp.sin(ac)], axis=-1)
    return cos_t, sin_t


def kernel(x, c, ctx, c_ctx, l0_w_mod, l0_b_mod, l0_norm1, l0_w_in, l0_mla_q_norm, l0_mla_w_q_up, l0_mla_kv_norm, l0_mla_w_kv_up, l0_mla_qk_q, l0_mla_qk_k, l0_na_qk_q, l0_na_qk_k, l0_na_rel_bias, l0_w_out, l0_norm2, l0_w_router, l0_b_router, l0_w_gate_up, l0_b_gate_up, l0_w_down, l0_b_down, l1_w_mod, l1_b_mod, l1_norm1, l1_w_in, l1_pool_w, l1_pool_scale, l1_conv_w, l1_w_out, l1_norm2, l1_w_router, l1_b_router, l1_w_gate_up, l1_b_gate_up, l1_w_down, l1_b_down):
    b, l, d = x.shape
    n_ctx = ctx.shape[1]
    n = b * l
    assert d % V7X_LANES == 0 and l % ROW_TILE == 0 and l % (NA_Q_ROWS * GRID_W) == 0
    assert l // GRID_W >= NA_K_ROWS and b + 1 <= V7X_SUBLANES
    x2 = x.reshape(n, d)

    def tiles_per_batch(tm):
        return l // tm

    c_rows = jnp.concatenate(
        [c, c_ctx[None], jnp.zeros((V7X_SUBLANES - b - 1, d), F32)], axis=0)
    perm = _rope_swap_perm()

    mod3 = _modulation(c_rows, l0_w_mod, l0_b_mod).reshape(V7X_SUBLANES, 1, 6 * d)
    kr0 = MLA_Q_RANK + MLA_KV_RANK
    kr1 = kr0 + MLA_ROPE
    w_in = jnp.concatenate(
        [l0_w_in[:, :kr1], l0_w_in[:, kr0:kr1][:, perm], l0_w_in[:, kr1:],
         jnp.zeros((d, U0_COLS - l0_w_in.shape[1] - MLA_ROPE), F32)], axis=1).astype(BF16)
    tpb = tiles_per_batch(ROW_TILE)
    u = _norm_proj(x2, l0_norm1, mod3, 0, 1, lambda i: i // tpb, w_in, ROW_TILE)
    ctx_tile = n_ctx if (b * n_ctx) % ROW_TILE else ROW_TILE
    u_ctx = _norm_proj(ctx.reshape(b * n_ctx, d), l0_norm1, mod3, 0, 1, lambda i: b, w_in, ctx_tile)

    wq = l0_mla_w_q_up.reshape(MLA_Q_RANK, MLA_HEADS, MLA_QK)
    wq = jnp.concatenate([wq, wq[:, :, MLA_NOPE:][:, :, perm]], axis=-1)
    wq = wq.transpose(1, 0, 2).astype(BF16)
    gq = jnp.concatenate([l0_mla_qk_q, l0_mla_qk_q[MLA_NOPE:][perm]]).reshape(1, -1)
    gk = jnp.concatenate([l0_mla_qk_k, l0_mla_qk_k[MLA_NOPE:][perm]]).reshape(1, -1)
    wkv = l0_mla_w_kv_up.reshape(MLA_KV_RANK, MLA_HEADS, MLA_NOPE + MLA_V)
    wkv = wkv.transpose(1, 0, 2).astype(BF16)
    cos_t, sin_t = _rope_tables(l)
    cos_k = jnp.concatenate([cos_t, jnp.ones((n_ctx, MLA_ROPE), F32)], axis=0)
    sin_k = jnp.concatenate([sin_t, jnp.zeros((n_ctx, MLA_ROPE), F32)], axis=0)
    q = _q_prep(u, b, l, l0_mla_q_norm, wq, gq, cos_t, sin_t, ROW_TILE)
    u3 = u.reshape(b, l, U0_COLS)
    uctx3 = u_ctx.reshape(b, n_ctx, U0_COLS)
    ckv0 = U0_CKV_BLK * V7X_LANES
    ckv1 = ckv0 + MLA_KV_RANK + 2 * MLA_ROPE
    ckv = jnp.concatenate([u3[:, :, ckv0:ckv1], uctx3[:, :, ckv0:ckv1]], axis=1)
    k_m, v_m = _kv_prep(ckv, l0_mla_kv_norm, wkv, gk, cos_k, sin_k, n_ctx)
    o_m = _mla_attention(q, k_m, v_m, ROW_TILE)

    bias = _na_bias_table(l0_na_rel_bias, l // GRID_W)
    o_n = _na_attention(u3, uctx3, l0_na_qk_q, l0_na_qk_k, bias)

    x2 = _attn_out(o_m.reshape(n, -1), o_n.reshape(n, -1), l0_w_out.astype(BF16), x2, mod3, 2,
                   tpb, ROW_TILE)
    x2 = _moe(x2, l0_norm2, mod3, tiles_per_batch, l0_w_router, l0_b_router, l0_w_gate_up,
              l0_b_gate_up, l0_w_down, l0_b_down)

    mod3 = _modulation(c_rows, l1_w_mod, l1_b_mod).reshape(V7X_SUBLANES, 1, 6 * d)
    u = _norm_proj(x2, l1_norm1, mod3, 0, 1, lambda i: i // tpb, l1_w_in.astype(BF16), ROW_TILE)
    x2 = _pool_conv_mixer(u, l1_pool_w, l1_pool_scale, l1_conv_w, l1_w_out.astype(BF16), x2, mod3,
                          2, l, ROW_TILE)
    x2 = _moe(x2, l1_norm2, mod3, tiles_per_batch, l1_w_router, l1_b_router, l1_w_gate_up,
              l1_b_gate_up, l1_w_down, l1_b_down)
    return x2.reshape(b, l, d)
```

```python
import functools

import numpy as np
import jax
import jax.numpy as jnp
from jax import lax
from jax.experimental import pallas as pl
from jax.experimental.pallas import tpu as pltpu

F32 = jnp.float32
BF16 = jnp.bfloat16

V7X_LANES = 128
V7X_SUBLANES = 8
V7X_VMEM_BYTES = 64 * 2**20
VMEM_LIMIT_BYTES = V7X_VMEM_BYTES * 7 // 8

GRID_W = 64
NORM_EPS = 1e-6
ROPE_BASE = 10000.0
MLA_HEADS = 8
MLA_Q_RANK = 512
MLA_KV_RANK = 256
MLA_NOPE = 128
MLA_ROPE = 64
MLA_V = 128
MLA_QK = MLA_NOPE + MLA_ROPE
MLA_QK_PAD = 256
NA_HEADS = 8
NA_DIM = 128
NA_WIN_ROWS = 8
NA_WIN_COLS = 16
NA_Q_ROWS = 8
NA_K_ROWS = 16
POOL_WINDOWS = (2, 4, 8, 16)
POOL_WIDTH = 1024
POOL_GROUP = POOL_WIDTH // len(POOL_WINDOWS)
CONV_CH = 1024
HALO = 16
N_EXPERTS = 32
TOP_K = 4
SWIGLU_LIMIT = 7.0
SWIGLU_ALPHA = 1.702
NEG_BIG = -1e30
LOG2_E = 1.4426950408889634

U0_COLS = 4096
U0_CKV_BLK = 4
U0_NAQ_BLK = 7
U0_NAK_BLK = 15
U0_NAV_BLK = 23

ROW_TILE = 512
MOE_ROWS = 512
MOE_F_TILE = 512
COMBINE_TOKENS = 256


def _cparams(*sem):
    return pltpu.CompilerParams(dimension_semantics=sem, vmem_limit_bytes=VMEM_LIMIT_BYTES)


def _rms(x, gain):
    ms = jnp.mean(x * x, axis=-1, keepdims=True)
    return x * lax.rsqrt(ms + NORM_EPS) * gain


def _dot_nt(a, b):
    return lax.dot_general(a, b, (((1,), (1,)), ((), ())), preferred_element_type=F32)


def _row_pitch(s):
    p = -(-s // V7X_SUBLANES) * V7X_SUBLANES
    return p + V7X_SUBLANES if p % (2 * V7X_SUBLANES) == 0 else p


def _mod_kernel(c_ref, w_ref, b_ref, o_ref):
    c = c_ref[...]
    s = c * jax.nn.sigmoid(c)
    o_ref[...] = jnp.dot(s, w_ref[...], preferred_element_type=F32,
                         precision=lax.Precision.HIGHEST) + b_ref[...]


def _modulation(c_rows, w_mod, b_mod):
    d, n = w_mod.shape
    tn = 512
    assert n % tn == 0
    return pl.pallas_call(
        _mod_kernel,
        out_shape=jax.ShapeDtypeStruct((V7X_SUBLANES, n), F32),
        grid=(n // tn,),
        in_specs=[pl.BlockSpec((V7X_SUBLANES, d), lambda j: (0, 0)),
                  pl.BlockSpec((d, tn), lambda j: (0, j)),
                  pl.BlockSpec((1, tn), lambda j: (0, j))],
        out_specs=pl.BlockSpec((V7X_SUBLANES, tn), lambda j: (0, j)),
        compiler_params=_cparams("parallel"),
        name="modulation",
    )(c_rows, w_mod, b_mod.reshape(1, n))


def _norm_proj_kernel(x_ref, g_ref, sh_ref, sc_ref, w_ref, o_ref, h_scr):
    @pl.when(pl.program_id(1) == 0)
    def _():
        h = _rms(x_ref[...], g_ref[...]) * (1.0 + sc_ref[...]) + sh_ref[...]
        h_scr[...] = h.astype(BF16)

    o_ref[...] = jnp.dot(h_scr[...], w_ref[...], preferred_element_type=F32).astype(o_ref.dtype)


def _norm_proj(x2, gain, mod3, shift_blk, scale_blk, group_of_tile, w_bf16, tm):
    n, d = x2.shape
    ncol = w_bf16.shape[1]
    tn = 1024
    return pl.pallas_call(
        _norm_proj_kernel,
        out_shape=jax.ShapeDtypeStruct((n, ncol), BF16),
        grid=(n // tm, ncol // tn),
        in_specs=[pl.BlockSpec((tm, d), lambda i, j: (i, 0)),
                  pl.BlockSpec((1, d), lambda i, j: (0, 0)),
                  pl.BlockSpec((None, 1, d), lambda i, j: (group_of_tile(i), 0, shift_blk)),
                  pl.BlockSpec((None, 1, d), lambda i, j: (group_of_tile(i), 0, scale_blk)),
                  pl.BlockSpec((d, tn), lambda i, j: (0, j))],
        out_specs=pl.BlockSpec((tm, tn), lambda i, j: (i, j)),
        scratch_shapes=[pltpu.VMEM((tm, d), BF16)],
        compiler_params=_cparams("parallel", "arbitrary"),
        name="norm_proj",
    )(x2, gain.reshape(1, d), mod3, mod3, w_bf16)


def _lane_iota(shape):
    return lax.broadcasted_iota(jnp.int32, shape, len(shape) - 1)


def _q_prep_kernel(cq_ref, qn_ref, wq_ref, g_ref, cos_ref, sin_ref, o_ref, cn_scr):
    @pl.when(pl.program_id(2) == 0)
    def _():
        cn_scr[...] = _rms(cq_ref[...].astype(F32), qn_ref[...]).astype(BF16)

    t = jnp.dot(cn_scr[...], wq_ref[...], preferred_element_type=F32)
    sq = jnp.where(_lane_iota(t.shape) < MLA_QK, t * t, 0.0)
    r = lax.rsqrt(jnp.sum(sq, axis=-1, keepdims=True) * (1.0 / MLA_QK) + NORM_EPS)
    tn = t * r * g_ref[...]
    rope = tn[:, MLA_NOPE:MLA_QK] * cos_ref[...] + tn[:, MLA_QK:] * sin_ref[...]
    out = jnp.concatenate([tn[:, :MLA_NOPE], rope, jnp.zeros_like(rope)], axis=-1)
    o_ref[...] = (out * (MLA_QK ** -0.5 * LOG2_E)).astype(o_ref.dtype)


def _q_prep(u, b, l, q_norm, wq, gq, cos_t, sin_t, tm):
    nt = l // tm
    return pl.pallas_call(
        _q_prep_kernel,
        out_shape=jax.ShapeDtypeStruct((b, MLA_HEADS, l, MLA_QK_PAD), BF16),
        grid=(b, nt, MLA_HEADS),
        in_specs=[pl.BlockSpec((tm, MLA_Q_RANK), lambda bi, i, h: (bi * nt + i, 0)),
                  pl.BlockSpec((1, MLA_Q_RANK), lambda bi, i, h: (0, 0)),
                  pl.BlockSpec((None, MLA_Q_RANK, MLA_QK_PAD), lambda bi, i, h: (h, 0, 0)),
                  pl.BlockSpec((1, MLA_QK_PAD), lambda bi, i, h: (0, 0)),
                  pl.BlockSpec((tm, MLA_ROPE), lambda bi, i, h: (i, 0)),
                  pl.BlockSpec((tm, MLA_ROPE), lambda bi, i, h: (i, 0))],
        out_specs=pl.BlockSpec((None, None, tm, MLA_QK_PAD), lambda bi, i, h: (bi, h, i, 0)),
        scratch_shapes=[pltpu.VMEM((tm, MLA_Q_RANK), BF16)],
        compiler_params=_cparams("parallel", "parallel", "arbitrary"),
        name="mla_q_prep",
    )(u, q_norm.reshape(1, -1), wq, gq, cos_t, sin_t)


def _kv_prep_kernel(ckv_ref, kvn_ref, wkv_ref, g_ref, cos_ref, sin_ref, k_ref, v_ref, cn_scr):
    @pl.when(pl.program_id(2) == 0)
    def _():
        cn_scr[...] = _rms(ckv_ref[:, :MLA_KV_RANK].astype(F32), kvn_ref[...]).astype(BF16)

    t = jnp.dot(cn_scr[...], wkv_ref[...], preferred_element_type=F32)
    kn = t[:, :MLA_NOPE]
    kr = ckv_ref[:, MLA_KV_RANK:].astype(F32)
    ss = (jnp.sum(kn * kn, axis=-1, keepdims=True)
          + jnp.sum(jnp.where(_lane_iota(kr.shape) < MLA_ROPE, kr * kr, 0.0), axis=-1, keepdims=True))
    r = lax.rsqrt(ss * (1.0 / MLA_QK) + NORM_EPS)
    g = g_ref[...]
    kn_n = kn * r * g[:, :MLA_NOPE]
    kr_n = kr * r * g[:, MLA_NOPE:]
    rope = kr_n[:, :MLA_ROPE] * cos_ref[...] + kr_n[:, MLA_ROPE:] * sin_ref[...]
    k_ref[...] = jnp.concatenate([kn_n, rope, jnp.zeros_like(rope)], axis=-1).astype(k_ref.dtype)
    v_ref[...] = t[:, MLA_NOPE:].astype(v_ref.dtype)


def _kv_prep(ckv, kv_norm, wkv, gk, cos_t, sin_t, tm):
    b, m, w = ckv.shape
    return pl.pallas_call(
        _kv_prep_kernel,
        out_shape=(jax.ShapeDtypeStruct((b, MLA_HEADS, m, MLA_QK_PAD), BF16),
                   jax.ShapeDtypeStruct((b, MLA_HEADS, m, MLA_V), BF16)),
        grid=(b, m // tm, MLA_HEADS),
        in_specs=[pl.BlockSpec((None, tm, w), lambda bi, i, h: (bi, i, 0)),
                  pl.BlockSpec((1, MLA_KV_RANK), lambda bi, i, h: (0, 0)),
                  pl.BlockSpec((None, MLA_KV_RANK, MLA_NOPE + MLA_V), lambda bi, i, h: (h, 0, 0)),
                  pl.BlockSpec((1, MLA_QK_PAD), lambda bi, i, h: (0, 0)),
                  pl.BlockSpec((tm, MLA_ROPE), lambda bi, i, h: (i, 0)),
                  pl.BlockSpec((tm, MLA_ROPE), lambda bi, i, h: (i, 0))],
        out_specs=(pl.BlockSpec((None, None, tm, MLA_QK_PAD), lambda bi, i, h: (bi, h, i, 0)),
                   pl.BlockSpec((None, None, tm, MLA_V), lambda bi, i, h: (bi, h, i, 0))),
        scratch_shapes=[pltpu.VMEM((tm, MLA_KV_RANK), BF16)],
        compiler_params=_cparams("parallel", "parallel", "arbitrary"),
        name="mla_kv_prep",
    )(ckv, kv_norm.reshape(1, -1), wkv, gk, cos_t, sin_t)


def _mla_kernel(q_ref, k_ref, v_ref, o_ref, sa_scr, sb_scr, m_scr, l_scr, acc_scr, *, tk, nk):
    m_scr[...] = jnp.full_like(m_scr, NEG_BIG)
    l_scr[...] = jnp.zeros_like(l_scr)
    acc_scr[...] = jnp.zeros_like(acc_scr)

    def scores(c, dst):
        off = pl.multiple_of(c * tk, tk)
        dst[...] = _dot_nt(q_ref[...], k_ref[pl.ds(off, tk), :])

    def consume(c, src):
        off = pl.multiple_of(c * tk, tk)
        s = src[...]
        m_prev = m_scr[...]
        m_new = jnp.maximum(m_prev, jnp.max(s, axis=-1, keepdims=True))
        alpha = jnp.exp2(m_prev - m_new)
        p = jnp.exp2(s - m_new)
        l_scr[...] = alpha * l_scr[...] + jnp.sum(p, axis=-1, keepdims=True)
        acc_scr[...] = alpha * acc_scr[...] + jnp.dot(
            p.astype(BF16), v_ref[pl.ds(off, tk), :], preferred_element_type=F32)
        m_scr[...] = m_new

    scores(0, sa_scr)

    def body(pair, carry):
        c = 2 * pair
        scores(c + 1, sb_scr)
        consume(c, sa_scr)
        scores(c + 2, sa_scr)
        consume(c + 1, sb_scr)
        return carry

    lax.fori_loop(0, (nk - 1) // 2, body, 0)
    if nk % 2 == 1:
        consume(nk - 1, sa_scr)
    else:
        scores(nk - 1, sb_scr)
        consume(nk - 2, sa_scr)
        consume(nk - 1, sb_scr)
    o_ref[...] = (acc_scr[...] / l_scr[...]).astype(o_ref.dtype)


def _mla_attention(q, k, v, tq):
    b, h, l, _ = q.shape
    m = k.shape[2]
    tk = next(t for t in (1280, 1024, 768, 512, 256, 128) if m % t == 0)
    return pl.pallas_call(
        functools.partial(_mla_kernel, tk=tk, nk=m // tk),
        out_shape=jax.ShapeDtypeStruct((b, l, h * MLA_V), BF16),
        grid=(b, h, l // tq),
        in_specs=[pl.BlockSpec((None, None, tq, MLA_QK_PAD), lambda bi, hi, i: (bi, hi, i, 0)),
                  pl.BlockSpec((None, None, m, MLA_QK_PAD), lambda bi, hi, i: (bi, hi, 0, 0)),
                  pl.BlockSpec((None, None, m, MLA_V), lambda bi, hi, i: (bi, hi, 0, 0))],
        out_specs=pl.BlockSpec((None, tq, MLA_V), lambda bi, hi, i: (bi, i, hi)),
        scratch_shapes=[pltpu.VMEM((tq, tk), F32), pltpu.VMEM((tq, tk), F32),
                        pltpu.VMEM((tq, 1), F32), pltpu.VMEM((tq, 1), F32),
                        pltpu.VMEM((tq, MLA_V), F32)],
        compiler_params=_cparams("parallel", "parallel", "arbitrary"),
        name="mla_attention",
    )(q, k, v)


def _na_kernel(q_ref, k_ref, v_ref, kc_ref, vc_ref, gq_ref, gk_ref, bias_ref, o_ref, *, rows):
    rb = pl.program_id(2)
    k_row0 = jnp.clip(rb * NA_Q_ROWS - NA_WIN_ROWS // 2, 0, rows - NA_K_ROWS)
    off = pl.multiple_of(k_row0 * GRID_W, NA_WIN_ROWS // 2 * GRID_W)
    nk = NA_K_ROWS * GRID_W
    q = (_rms(q_ref[...].astype(F32), gq_ref[...]) * (NA_DIM ** -0.5)).astype(BF16)
    kw = _rms(k_ref[pl.ds(off, nk), :].astype(F32), gk_ref[...]).astype(BF16)
    kc = _rms(kc_ref[...].astype(F32), gk_ref[...]).astype(BF16)
    s_w = _dot_nt(q, kw) + bias_ref[...]
    s_c = _dot_nt(q, kc)
    m = jnp.maximum(jnp.max(s_w, axis=-1, keepdims=True), jnp.max(s_c, axis=-1, keepdims=True))
    p_w = jnp.exp(s_w - m)
    p_c = jnp.exp(s_c - m)
    denom = jnp.sum(p_w, axis=-1, keepdims=True) + jnp.sum(p_c, axis=-1, keepdims=True)
    o = (jnp.dot(p_w.astype(BF16), v_ref[pl.ds(off, nk), :], preferred_element_type=F32)
         + jnp.dot(p_c.astype(BF16), vc_ref[...], preferred_element_type=F32))
    o_ref[...] = (o / denom).astype(o_ref.dtype)


def _na_bias_table(rel_bias, rows):
    nrb = rows // NA_Q_ROWS
    tables = []
    for rb in (0, min(1, nrb - 1), nrb - 1):
        k_row0 = int(np.clip(rb * NA_Q_ROWS - NA_WIN_ROWS // 2, 0, rows - NA_K_ROWS))
        r = rb * NA_Q_ROWS + np.arange(NA_Q_ROWS)
        kr = k_row0 + np.arange(NA_K_ROWS)
        qc = np.arange(GRID_W)
        kc = np.arange(GRID_W)
        rs = np.clip(r - NA_WIN_ROWS // 2, 0, rows - NA_WIN_ROWS)
        cs = np.clip(qc - NA_WIN_COLS // 2, 0, GRID_W - NA_WIN_COLS)
        row_ok = (kr[None, :] >= rs[:, None]) & (kr[None, :] < rs[:, None] + NA_WIN_ROWS)
        col_ok = (kc[None, :] >= cs[:, None]) & (kc[None, :] < cs[:, None] + NA_WIN_COLS)
        ri = np.clip(kr[None, :] - r[:, None] + NA_WIN_ROWS - 1, 0, 2 * NA_WIN_ROWS - 2)
        ci = np.clip(kc[None, :] - qc[:, None] + NA_WIN_COLS - 1, 0, 2 * NA_WIN_COLS - 2)
        sel_r = (ri.reshape(-1)[:, None] == np.arange(2 * NA_WIN_ROWS - 1)[None, :]).astype(np.float32)
        sel_c = (np.arange(2 * NA_WIN_COLS - 1)[:, None] == ci.reshape(-1)[None, :]).astype(np.float32)
        bias = jnp.einsum("rm,hmc->hrc", sel_r, rel_bias.astype(F32), precision=lax.Precision.HIGHEST)
        bias = jnp.einsum("hrc,cq->hrq", bias, sel_c, precision=lax.Precision.HIGHEST)
        bias = bias.reshape(NA_HEADS, NA_Q_ROWS, NA_K_ROWS, GRID_W, GRID_W).transpose(0, 1, 3, 2, 4)
        ok = row_ok[:, None, :, None] & col_ok[None, :, None, :]
        bias = jnp.where(ok[None], bias.astype(F32), NEG_BIG)
        tables.append(bias.reshape(NA_HEADS, NA_Q_ROWS * GRID_W, NA_K_ROWS * GRID_W))
    return jnp.stack(tables)


def _na_attention(u3, uctx3, gq, gk, bias):
    b, l, _ = u3.shape
    n_ctx = uctx3.shape[1]
    rows = l // GRID_W
    nrb = rows // NA_Q_ROWS
    tq = NA_Q_ROWS * GRID_W

    def pattern(rb):
        return jnp.where(rb == 0, 0, jnp.where(rb == nrb - 1, 2, 1))

    return pl.pallas_call(
        functools.partial(_na_kernel, rows=rows),
        out_shape=jax.ShapeDtypeStruct((b, l, NA_HEADS * NA_DIM), BF16),
        grid=(b, NA_HEADS, nrb),
        in_specs=[pl.BlockSpec((None, tq, NA_DIM), lambda bi, h, rb: (bi, rb, U0_NAQ_BLK + h)),
                  pl.BlockSpec((None, l, NA_DIM), lambda bi, h, rb: (bi, 0, U0_NAK_BLK + h)),
                  pl.BlockSpec((None, l, NA_DIM), lambda bi, h, rb: (bi, 0, U0_NAV_BLK + h)),
                  pl.BlockSpec((None, n_ctx, NA_DIM), lambda bi, h, rb: (bi, 0, U0_NAK_BLK + h)),
                  pl.BlockSpec((None, n_ctx, NA_DIM), lambda bi, h, rb: (bi, 0, U0_NAV_BLK + h)),
                  pl.BlockSpec((1, NA_DIM), lambda bi, h, rb: (0, 0)),
                  pl.BlockSpec((1, NA_DIM), lambda bi, h, rb: (0, 0)),
                  pl.BlockSpec((None, None, tq, NA_K_ROWS * GRID_W),
                               lambda bi, h, rb: (pattern(rb), h, 0, 0))],
        out_specs=pl.BlockSpec((None, tq, NA_DIM), lambda bi, h, rb: (bi, rb, h)),
        compiler_params=_cparams("parallel", "parallel", "arbitrary"),
        name="neighbourhood_attention",
    )(u3, u3, u3, uctx3, uctx3, gq.reshape(1, -1), gk.reshape(1, -1), bias)


def _attn_out_kernel(am_ref, an_ref, wm_ref, wn_ref, x_ref, gate_ref, o_ref):
    y = (jnp.dot(am_ref[...], wm_ref[...], preferred_element_type=F32)
         + jnp.dot(an_ref[...], wn_ref[...], preferred_element_type=F32))
    o_ref[...] = x_ref[...] + gate_ref[...] * y


def _attn_out(a_m, a_n, w_out_bf16, x2, mod3, gate_blk, tiles_per_batch, tm):
    n, d = x2.shape
    half = a_m.shape[1]
    return pl.pallas_call(
        _attn_out_kernel,
        out_shape=jax.ShapeDtypeStruct((n, d), F32),
        grid=(n // tm,),
        in_specs=[pl.BlockSpec((tm, half), lambda i: (i, 0)),
                  pl.BlockSpec((tm, half), lambda i: (i, 0)),
                  pl.BlockSpec((half, d), lambda i: (0, 0)),
                  pl.BlockSpec((half, d), lambda i: (1, 0)),
                  pl.BlockSpec((tm, d), lambda i: (i, 0)),
                  pl.BlockSpec((None, 1, d), lambda i: (i // tiles_per_batch, 0, gate_blk))],
        out_specs=pl.BlockSpec((tm, d), lambda i: (i, 0)),
        compiler_params=_cparams("parallel"),
        name="attn_out_residual",
    )(a_m, a_n, w_out_bf16, w_out_bf16, x2, mod3)


def _pool_conv_kernel(prev_ref, u_ref, next_ref, pw_ref, ps_ref, cw_ref, wp_ref, wc_ref, x_ref,
                      gate_ref, o_ref, *, seq_len, tiles_per_batch):
    tm = u_ref.shape[0]
    n_ext = tm + 2 * HALO
    pos0 = (pl.program_id(0) % tiles_per_batch) * tm
    pos_ext = pos0 - HALO + lax.broadcasted_iota(jnp.int32, (n_ext, 1), 0)
    valid = (pos_ext >= 0) & (pos_ext < seq_len)

    def ext(c0, c1):
        e = jnp.concatenate([prev_ref[:, c0:c1], u_ref[:, c0:c1], next_ref[:, c0:c1]], axis=0)
        return jnp.where(valid, e.astype(F32), 0.0)

    def shifted(a, d):
        return pltpu.roll(a, (-d) % n_ext, axis=0)

    pos = (pos0 + lax.broadcasted_iota(jnp.int32, (tm, 1), 0)).astype(F32)
    y = jnp.zeros((tm, x_ref.shape[1]), F32)
    for g, w in enumerate(POOL_WINDOWS):
        c0 = g * POOL_GROUP
        e = ext(c0, c0 + POOL_GROUP)
        acc = e + shifted(e, -1)
        span = 1
        while 2 * span < w:
            acc = shifted(acc, span) + shifted(acc, -span)
            span *= 2
        half = w // 2
        cnt = jnp.minimum(pos + half, float(seq_len)) - jnp.maximum(pos - half, 0.0)
        pooled = acc[HALO:HALO + tm] / cnt - e[HALO:HALO + tm]
        mixed = jnp.dot(pooled.astype(BF16), pw_ref[g], preferred_element_type=F32)
        y_g = (mixed * ps_ref[:, c0:c0 + POOL_GROUP]).astype(BF16)
        y = y + jnp.dot(y_g, wp_ref[c0:c0 + POOL_GROUP, :], preferred_element_type=F32)

    gb0 = POOL_WIDTH
    gc0 = POOL_WIDTH + CONV_CH
    v0 = POOL_WIDTH + 2 * CONV_CH
    z = ext(gc0, gc0 + CONV_CH) * ext(v0, v0 + CONV_CH)
    cw = cw_ref[...]
    conv = cw[0:1] * shifted(z, -1) + cw[1:2] * z + cw[2:3] * shifted(z, 1)
    y_conv = u_ref[:, gb0:gb0 + CONV_CH].astype(F32) * conv[HALO:HALO + tm]
    y = y + jnp.dot(y_conv.astype(BF16), wc_ref[...], preferred_element_type=F32)
    o_ref[...] = x_ref[...] + gate_ref[...] * y


def _pool_conv_mixer(u, pool_w, pool_scale, conv_w, w_out_bf16, x2, mod3, gate_blk, seq_len, tm):
    n, d = x2.shape
    width = u.shape[1]
    tiles_per_batch = seq_len // tm
    hb = tm // HALO
    last = n // HALO - 1
    return pl.pallas_call(
        functools.partial(_pool_conv_kernel, seq_len=seq_len, tiles_per_batch=tiles_per_batch),
        out_shape=jax.ShapeDtypeStruct((n, d), F32),
        grid=(n // tm,),
        in_specs=[pl.BlockSpec((HALO, width), lambda i: (jnp.maximum(i * hb - 1, 0), 0)),
                  pl.BlockSpec((tm, width), lambda i: (i, 0)),
                  pl.BlockSpec((HALO, width), lambda i: (jnp.minimum((i + 1) * hb, last), 0)),
                  pl.BlockSpec(pool_w.shape, lambda i: (0, 0, 0)),
                  pl.BlockSpec((1, POOL_WIDTH), lambda i: (0, 0)),
                  pl.BlockSpec((V7X_SUBLANES, CONV_CH), lambda i: (0, 0)),
                  pl.BlockSpec((POOL_WIDTH, d), lambda i: (0, 0)),
                  pl.BlockSpec((CONV_CH, d), lambda i: (1, 0)),
                  pl.BlockSpec((tm, d), lambda i: (i, 0)),
                  pl.BlockSpec((None, 1, d), lambda i: (i // tiles_per_batch, 0, gate_blk))],
        out_specs=pl.BlockSpec((tm, d), lambda i: (i, 0)),
        compiler_params=_cparams("parallel"),
        name="pool_conv_mixer",
    )(u, u, u, pool_w.astype(BF16), pool_scale.reshape(1, -1),
      jnp.pad(conv_w, ((0, V7X_SUBLANES - conv_w.shape[0]), (0, 0))), w_out_bf16, w_out_bf16, x2, mod3)


def _split_bf16(a):
    hi = a.astype(BF16)
    return hi, (a - hi.astype(F32)).astype(BF16)


def _pack_bf16_pairs(lo, hi):
    lo_bits = lax.bitcast_convert_type(lo.astype(BF16).astype(F32), jnp.uint32) >> 16
    hi_bits = lax.bitcast_convert_type(hi.astype(BF16).astype(F32), jnp.uint32) & jnp.uint32(0xFFFF0000)
    return hi_bits | lo_bits


def _unpack_bf16_pairs(w):
    lo = lax.bitcast_convert_type(w << 16, F32).astype(BF16)
    hi = lax.bitcast_convert_type(w & jnp.uint32(0xFFFF0000), F32).astype(BF16)
    return lo, hi


def _router_kernel(x_ref, g_ref, sh_ref, sc_ref, wr_ref, br_ref, tri_ref, hp_ref, idx_ref, gate_ref,
                   rank_ref, cnt_ref, run_scr, *, p_rows):
    tm, d = x_ref.shape

    @pl.when(pl.program_id(0) == 0)
    def _():
        run_scr[...] = jnp.zeros_like(run_scr)

    h = _rms(x_ref[...], g_ref[...]) * (1.0 + sc_ref[...]) + sh_ref[...]
    packed = _pack_bf16_pairs(h[:, :d // 2], h[:, d // 2:])
    for s in range(p_rows):
        hp_ref[pl.ds(s, tm, stride=p_rows), :] = packed[:, s * V7X_LANES:(s + 1) * V7X_LANES]
    h_hi, h_lo = _split_bf16(h)
    w_hi, w_lo = _split_bf16(wr_ref[...])
    logits = _dot_nt(w_hi, h_hi) + _dot_nt(w_hi, h_lo) + _dot_nt(w_lo, h_hi) + br_ref[...]
    expert = lax.broadcasted_iota(jnp.int32, logits.shape, 0)
    vals, ids = [], []
    for _ in range(TOP_K):
        mx = jnp.max(logits, axis=0, keepdims=True)
        am = jnp.min(jnp.where(logits == mx, expert, N_EXPERTS), axis=0, keepdims=True)
        vals.append(mx)
        ids.append(am)
        logits = jnp.where(expert == am, -jnp.inf, logits)
    ex = [jnp.exp(v - vals[0]) for v in vals]
    tot = ex[0] + ex[1] + ex[2] + ex[3]
    idx_ref[...] = jnp.concatenate(ids, axis=0)
    gate_ref[...] = jnp.concatenate([e / tot for e in ex], axis=0)

    running = run_scr[...]
    ranks = []
    for k in range(TOP_K):
        hit = expert == ids[k]
        onehot = jnp.where(hit, 1.0, 0.0)
        before = jnp.dot(onehot.astype(BF16), tri_ref[...], preferred_element_type=F32)
        rank = jnp.sum(jnp.where(hit, before + running, 0.0), axis=0, keepdims=True)
        ranks.append(rank.astype(jnp.int32))
        running = running + jnp.sum(onehot, axis=1, keepdims=True)
    run_scr[...] = running
    rank_ref[...] = jnp.concatenate(ranks, axis=0)
    cnt_ref[...] = jnp.broadcast_to(running, cnt_ref.shape).astype(jnp.int32)


def _router(x2, gain, mod3, shift_blk, scale_blk, tiles_per_batch, w_router, b_router, tm):
    n, d = x2.shape
    p_rows = d // 2 // V7X_LANES
    slot = jax.ShapeDtypeStruct((TOP_K, n), jnp.int32)
    slot_spec = pl.BlockSpec((TOP_K, tm), lambda i: (0, i))
    tri = jnp.triu(jnp.ones((tm, tm), BF16), k=1)
    return pl.pallas_call(
        functools.partial(_router_kernel, p_rows=p_rows),
        out_shape=(jax.ShapeDtypeStruct((n * p_rows, V7X_LANES), jnp.uint32),
                   slot, jax.ShapeDtypeStruct((TOP_K, n), F32), slot,
                   jax.ShapeDtypeStruct((N_EXPERTS, V7X_LANES), jnp.int32)),
        grid=(n // tm,),
        in_specs=[pl.BlockSpec((tm, d), lambda i: (i, 0)),
                  pl.BlockSpec((1, d), lambda i: (0, 0)),
                  pl.BlockSpec((None, 1, d), lambda i: (i // tiles_per_batch, 0, shift_blk)),
                  pl.BlockSpec((None, 1, d), lambda i: (i // tiles_per_batch, 0, scale_blk)),
                  pl.BlockSpec((N_EXPERTS, d), lambda i: (0, 0)),
                  pl.BlockSpec((N_EXPERTS, 1), lambda i: (0, 0)),
                  pl.BlockSpec((tm, tm), lambda i: (0, 0))],
        out_specs=(pl.BlockSpec((tm * p_rows, V7X_LANES), lambda i: (i, 0)),
                   slot_spec, slot_spec, slot_spec,
                   pl.BlockSpec((N_EXPERTS, V7X_LANES), lambda i: (0, 0))),
        scratch_shapes=[pltpu.VMEM((N_EXPERTS, 1), F32)],
        compiler_params=_cparams("arbitrary"),
        name="moe_router",
    )(x2, gain.reshape(1, d), mod3, mod3, w_router.T, b_router.reshape(-1, 1), tri)


def _slot_pos_kernel(pstart_ref, idx_ref, rank_ref, pos_ref):
    idx = idx_ref[...]
    pos = rank_ref[...]
    for e in range(N_EXPERTS):
        pos = pos + jnp.where(idx == e, pstart_ref[e], 0)
    pos_ref[...] = pos


def _slot_positions(pstart, idx_t, rank_t):
    k, n = idx_t.shape
    tn = next(t for t in (4096, 2048, 1024, 512) if n % t == 0)
    spec = pl.BlockSpec((k, tn), lambda i, ps: (0, i))
    return pl.pallas_call(
        _slot_pos_kernel,
        out_shape=jax.ShapeDtypeStruct((k, n), jnp.int32),
        grid_spec=pltpu.PrefetchScalarGridSpec(
            num_scalar_prefetch=1, grid=(n // tn,), in_specs=[spec, spec], out_specs=spec),
        compiler_params=_cparams("parallel"),
        name="moe_slot_positions",
    )(pstart, idx_t, rank_t)


def _dispatch_kernel(pos_hbm, hp_ref, xs_zero_hbm, xs_hbm, idx_smem, sem, idx_sem, *, p_rows):
    del xs_zero_hbm
    tt = hp_ref.shape[0] // p_rows
    cp = pltpu.make_async_copy(pos_hbm.at[pl.program_id(0)], idx_smem, idx_sem)
    cp.start()
    cp.wait()

    def row_copy(r, dst_row):
        t = r & (tt - 1)
        src = hp_ref.at[pl.ds(pl.multiple_of(t * p_rows, p_rows), p_rows)]
        return pltpu.make_async_copy(src, xs_hbm.at[dst_row], sem)

    def issue(r, c):
        row_copy(r, idx_smem[r]).start()
        return c

    def drain(r, c):
        row_copy(r, 0).wait()
        return c

    lax.fori_loop(0, idx_smem.shape[0], issue, 0, unroll=8)
    lax.fori_loop(0, idx_smem.shape[0], drain, 0, unroll=8)


def _dispatch(pos_tiles, h_packed, n_rows, p_rows):
    n_tiles, slots = pos_tiles.shape
    tt = slots // TOP_K
    assert tt & (tt - 1) == 0
    xs_shape = jax.ShapeDtypeStruct((n_rows, p_rows, V7X_LANES), jnp.uint32)
    return pl.pallas_call(
        functools.partial(_dispatch_kernel, p_rows=p_rows),
        out_shape=xs_shape,
        grid=(n_tiles,),
        in_specs=[pl.BlockSpec(memory_space=pl.ANY),
                  pl.BlockSpec((tt * p_rows, V7X_LANES), lambda i: (i, 0)),
                  pl.BlockSpec(memory_space=pl.ANY)],
        out_specs=pl.BlockSpec(memory_space=pl.ANY),
        scratch_shapes=[pltpu.SMEM((slots,), jnp.int32),
                        pltpu.SemaphoreType.DMA,
                        pltpu.SemaphoreType.DMA],
        input_output_aliases={2: 0},
        compiler_params=_cparams("arbitrary"),
        name="moe_dispatch",
    )(pos_tiles, h_packed, jnp.zeros(xs_shape.shape, xs_shape.dtype))


def _gather_rows(idx_hbm_row, src_hbm, idx_smem, buf, sem, idx_sem, n, s_rows, pitch):
    cp = pltpu.make_async_copy(idx_hbm_row, idx_smem, idx_sem)
    cp.start()
    cp.wait()

    def row_copy(r, src_row):
        dst = buf.at[pl.ds(pl.multiple_of(r * pitch, V7X_SUBLANES), s_rows)]
        return pltpu.make_async_copy(src_hbm.at[src_row], dst, sem)

    def issue(r, c):
        row_copy(r, idx_smem[r]).start()
        return c

    def drain(r, c):
        row_copy(r, 0).wait()
        return c

    lax.fori_loop(0, n, issue, 0, unroll=8)
    lax.fori_loop(0, n, drain, 0, unroll=8)


def _experts_kernel(be_ref, nused_ref, xs_ref, wg_ref, wu_ref, bg_ref, bu_ref, wd_ref, bd_ref,
                    y_ref, x_scr, acc_scr, *, p_rows, s_rows):
    i = pl.program_id(0)
    j = pl.program_id(1)
    tb, d = x_scr.shape
    active = i < nused_ref[0]

    @pl.when(active & (j == 0))
    def _():
        for s in range(p_rows):
            lo, hi = _unpack_bf16_pairs(xs_ref[pl.ds(s, tb, stride=p_rows), :])
            x_scr[:, s * V7X_LANES:(s + 1) * V7X_LANES] = lo
            x_scr[:, d // 2 + s * V7X_LANES:d // 2 + (s + 1) * V7X_LANES] = hi
        acc_scr[...] = jnp.zeros_like(acc_scr)

    @pl.when(active)
    def _():
        x = x_scr[...]
        a_gate = jnp.dot(x, wg_ref[...], preferred_element_type=F32) + bg_ref[...]
        a_up = jnp.dot(x, wu_ref[...], preferred_element_type=F32) + bu_ref[...]
        a_gate = jnp.minimum(a_gate, SWIGLU_LIMIT)
        a_up = jnp.clip(a_up, -SWIGLU_LIMIT, SWIGLU_LIMIT)
        act = (a_up + 1.0) * (a_gate * jax.nn.sigmoid(SWIGLU_ALPHA * a_gate))
        acc_scr[...] += jnp.dot(act.astype(BF16), wd_ref[...], preferred_element_type=F32)

    last = j == pl.num_programs(1) - 1

    @pl.when(active & last)
    def _():
        y = acc_scr[...] + bd_ref[...]
        for s in range(s_rows):
            y_ref[pl.ds(s, tb, stride=s_rows), :] = y[:, s * V7X_LANES:(s + 1) * V7X_LANES]

    @pl.when(jnp.logical_not(active) & last)
    def _():
        y_ref[...] = jnp.zeros_like(y_ref)


def _experts(block_expert, n_used, xs_rows, tb, w_gate_up, b_gate_up, w_down, b_down):
    e, d, two_f = w_gate_up.shape
    p_rows = d // 2 // V7X_LANES
    n_blocks = xs_rows.shape[0] // (tb * p_rows)
    f = two_f // 2
    tf = MOE_F_TILE
    nf = f // tf
    s_rows = d // V7X_LANES

    def fcol(i, j, nused):
        return jnp.where(i < nused[0], j, nf - 1)

    grid_spec = pltpu.PrefetchScalarGridSpec(
        num_scalar_prefetch=2,
        grid=(n_blocks, nf),
        in_specs=[pl.BlockSpec((tb * p_rows, V7X_LANES), lambda i, j, be, nu: (i, 0)),
                  pl.BlockSpec((None, d, tf), lambda i, j, be, nu: (be[i], 0, fcol(i, j, nu))),
                  pl.BlockSpec((None, d, tf), lambda i, j, be, nu: (be[i], 0, nf + fcol(i, j, nu))),
                  pl.BlockSpec((None, 1, tf), lambda i, j, be, nu: (be[i], 0, fcol(i, j, nu))),
                  pl.BlockSpec((None, 1, tf), lambda i, j, be, nu: (be[i], 0, nf + fcol(i, j, nu))),
                  pl.BlockSpec((None, tf, d), lambda i, j, be, nu: (be[i], fcol(i, j, nu), 0)),
                  pl.BlockSpec((None, 1, d), lambda i, j, be, nu: (be[i], 0, 0))],
        out_specs=pl.BlockSpec((tb * s_rows, V7X_LANES), lambda i, j, be, nu: (i, 0)),
        scratch_shapes=[pltpu.VMEM((tb, d), BF16),
                        pltpu.VMEM((tb, d), F32)])
    return pl.pallas_call(
        functools.partial(_experts_kernel, p_rows=p_rows, s_rows=s_rows),
        out_shape=jax.ShapeDtypeStruct((n_blocks * tb * s_rows, V7X_LANES), F32),
        grid_spec=grid_spec,
        compiler_params=_cparams("arbitrary", "arbitrary"),
        name="moe_experts",
    )(block_expert, n_used, xs_rows, w_gate_up, w_gate_up,
      b_gate_up.reshape(e, 1, two_f), b_gate_up.reshape(e, 1, two_f), w_down,
      b_down.reshape(e, 1, d))


def _combine_kernel(pos_hbm, y_hbm, x_ref, mgate_ref, rgate_ref, o_ref, idx_smem, gbuf, sem,
                    idx_sem, *, s_rows, pitch):
    tt = x_ref.shape[0]
    _gather_rows(pos_hbm.at[pl.program_id(0)], y_hbm, idx_smem, gbuf, sem, idx_sem,
                 TOP_K * tt, s_rows, pitch)
    rg = rgate_ref[...]
    for s in range(s_rows):
        tot = rg[:, 0:1] * gbuf[pl.ds(s, tt, stride=pitch), :]
        for k in range(1, TOP_K):
            tot = tot + rg[:, k:k + 1] * gbuf[pl.ds(k * tt * pitch + s, tt, stride=pitch), :]
        cols = slice(s * V7X_LANES, (s + 1) * V7X_LANES)
        o_ref[:, cols] = x_ref[:, cols] + mgate_ref[:, cols] * tot


def _combine(pos_tiles, y_rows, x2, mod3, gate_blk, router_gates, tiles_per_batch, tt):
    n, d = x2.shape
    s_rows = d // V7X_LANES
    pitch = _row_pitch(s_rows)
    return pl.pallas_call(
        functools.partial(_combine_kernel, s_rows=s_rows, pitch=pitch),
        out_shape=jax.ShapeDtypeStruct((n, d), F32),
        grid=(n // tt,),
        in_specs=[pl.BlockSpec(memory_space=pl.ANY),
                  pl.BlockSpec(memory_space=pl.ANY),
                  pl.BlockSpec((tt, d), lambda i: (i, 0)),
                  pl.BlockSpec((None, 1, d), lambda i: (i // tiles_per_batch, 0, gate_blk)),
                  pl.BlockSpec((tt, TOP_K), lambda i: (i, 0))],
        out_specs=pl.BlockSpec((tt, d), lambda i: (i, 0)),
        scratch_shapes=[pltpu.SMEM((TOP_K * tt,), jnp.int32),
                        pltpu.VMEM((TOP_K * tt * pitch, V7X_LANES), F32),
                        pltpu.SemaphoreType.DMA,
                        pltpu.SemaphoreType.DMA],
        compiler_params=_cparams("arbitrary"),
        name="moe_combine",
    )(pos_tiles, y_rows, x2, mod3, router_gates)


def _block_tables(counts, n_slots, tb):
    padded = (counts + tb - 1) // tb * tb
    pend = jnp.cumsum(padded)
    pstart = (pend - padded).astype(jnp.int32)
    n_blocks = -(-n_slots // tb) + N_EXPERTS
    block_expert = jnp.minimum(
        jnp.searchsorted(pend, jnp.arange(n_blocks, dtype=jnp.int32) * tb, side="right"),
        N_EXPERTS - 1).astype(jnp.int32)
    n_used = (pend[-1] // tb).astype(jnp.int32).reshape(1)
    return pstart, block_expert, n_used, n_blocks


def _moe(x2, norm2, mod3, tiles_per_batch_fn, w_router, b_router, w_gate_up, b_gate_up, w_down,
         b_down):
    n, d = x2.shape
    s_rows = d // V7X_LANES
    p_rows = d // 2 // V7X_LANES
    h_packed, idx_t, gate_t, rank_t, counts = _router(
        x2, norm2, mod3, 3, 4, tiles_per_batch_fn(ROW_TILE), w_router, b_router, ROW_TILE)
    pstart, block_expert, n_used, n_blocks = _block_tables(counts[:, 0], TOP_K * n, MOE_ROWS)
    pos = _slot_positions(pstart, idx_t, rank_t)
    tt = COMBINE_TOKENS
    pos_tiles = pos.reshape(TOP_K, n // tt, tt).transpose(1, 0, 2).reshape(n // tt, TOP_K * tt)
    xs = _dispatch(pos_tiles, h_packed, n_blocks * MOE_ROWS, p_rows)
    y_rows = _experts(block_expert, n_used, xs.reshape(-1, V7X_LANES), MOE_ROWS,
                      w_gate_up.astype(BF16), b_gate_up, w_down.astype(BF16), b_down)
    return _combine(pos_tiles, y_rows.reshape(-1, s_rows, V7X_LANES), x2, mod3, 5, gate_t.T,
                    tiles_per_batch_fn(tt), tt)


def _rope_swap_perm():
    half = MLA_ROPE // 2
    quarter = half // 2
    j = np.arange(MLA_ROPE)
    return (j // half) * half + (j % half + quarter) % half


def _rope_tables(n_tokens):
    t = jnp.arange(n_tokens, dtype=jnp.int32)
    row = (t // GRID_W).astype(F32)
    col = (t % GRID_W).astype(F32)
    n_freq = MLA_ROPE // 4
    inv_freq = ROPE_BASE ** (-jnp.arange(n_freq, dtype=F32) / n_freq)
    ar = row[:, None] * inv_freq
    ac = col[:, None] * inv_freq
    cos_t = jnp.concatenate([jnp.cos(ar), jnp.cos(ar), jnp.cos(ac), jnp.cos(ac)], axis=-1)
    sin_t = jnp.concatenate([-jnp.sin(ar), jnp.sin(ar), -jnp.sin(ac), jnp.sin(ac)], axis=-1)
    return cos_t, sin_t


def kernel(x, c, ctx, c_ctx, l0_w_mod, l0_b_mod, l0_norm1, l0_w_in, l0_mla_q_norm, l0_mla_w_q_up, l0_mla_kv_norm, l0_mla_w_kv_up, l0_mla_qk_q, l0_mla_qk_k, l0_na_qk_q, l0_na_qk_k, l0_na_rel_bias, l0_w_out, l0_norm2, l0_w_router, l0_b_router, l0_w_gate_up, l0_b_gate_up, l0_w_down, l0_b_down, l1_w_mod, l1_b_mod, l1_norm1, l1_w_in, l1_pool_w, l1_pool_scale, l1_conv_w, l1_w_out, l1_norm2, l1_w_router, l1_b_router, l1_w_gate_up, l1_b_gate_up, l1_w_down, l1_b_down):
    b, l, d = x.shape
    n_ctx = ctx.shape[1]
    n = b * l
    assert d % V7X_LANES == 0 and l % ROW_TILE == 0 and l % (NA_Q_ROWS * GRID_W) == 0
    assert l // GRID_W >= NA_K_ROWS and b + 1 <= V7X_SUBLANES
    x2 = x.reshape(n, d)

    def tiles_per_batch(tm):
        return l // tm

    c_rows = jnp.concatenate(
        [c, c_ctx[None], jnp.zeros((V7X_SUBLANES - b - 1, d), F32)], axis=0)
    perm = _rope_swap_perm()

    mod3 = _modulation(c_rows, l0_w_mod, l0_b_mod).reshape(V7X_SUBLANES, 1, 6 * d)
    kr0 = MLA_Q_RANK + MLA_KV_RANK
    kr1 = kr0 + MLA_ROPE
    w_in = jnp.concatenate(
        [l0_w_in[:, :kr1], l0_w_in[:, kr0:kr1][:, perm], l0_w_in[:, kr1:],
         jnp.zeros((d, U0_COLS - l0_w_in.shape[1] - MLA_ROPE), F32)], axis=1).astype(BF16)
    tpb = tiles_per_batch(ROW_TILE)
    u = _norm_proj(x2, l0_norm1, mod3, 0, 1, lambda i: i // tpb, w_in, ROW_TILE)
    ctx_tile = n_ctx if (b * n_ctx) % ROW_TILE else ROW_TILE
    u_ctx = _norm_proj(ctx.reshape(b * n_ctx, d), l0_norm1, mod3, 0, 1, lambda i: b, w_in, ctx_tile)

    wq = l0_mla_w_q_up.reshape(MLA_Q_RANK, MLA_HEADS, MLA_QK)
    wq = jnp.concatenate([wq, wq[:, :, MLA_NOPE:][:, :, perm]], axis=-1)
    wq = wq.transpose(1, 0, 2).astype(BF16)
    gq = jnp.concatenate([l0_mla_qk_q, l0_mla_qk_q[MLA_NOPE:][perm]]).reshape(1, -1)
    gk = jnp.concatenate([l0_mla_qk_k, l0_mla_qk_k[MLA_NOPE:][perm]]).reshape(1, -1)
    wkv = l0_mla_w_kv_up.reshape(MLA_KV_RANK, MLA_HEADS, MLA_NOPE + MLA_V)
    wkv = wkv.transpose(1, 0, 2).astype(BF16)
    cos_t, sin_t = _rope_tables(l)
    cos_k = jnp.concatenate([cos_t, jnp.ones((n_ctx, MLA_ROPE), F32)], axis=0)
    sin_k = jnp.concatenate([sin_t, jnp.zeros((n_ctx, MLA_ROPE), F32)], axis=0)
    q = _q_prep(u, b, l, l0_mla_q_norm, wq, gq, cos_t, sin_t, ROW_TILE)
    u3 = u.reshape(b, l, U0_COLS)
    uctx3 = u_ctx.reshape(b, n_ctx, U0_COLS)
    ckv0 = U0_CKV_BLK * V7X_LANES
    ckv1 = ckv0 + MLA_KV_RANK + 2 * MLA_ROPE
    ckv = jnp.concatenate([u3[:, :, ckv0:ckv1], uctx3[:, :, ckv0:ckv1]], axis=1)
    k_m, v_m = _kv_prep(ckv, l0_mla_kv_norm, wkv, gk, cos_k, sin_k, n_ctx)
    o_m = _mla_attention(q, k_m, v_m, ROW_TILE)

    bias = _na_bias_table(l0_na_rel_bias, l // GRID_W)
    o_n = _na_attention(u3, uctx3, l0_na_qk_q, l0_na_qk_k, bias)

    x2 = _attn_out(o_m.reshape(n, -1), o_n.reshape(n, -1), l0_w_out.astype(BF16), x2, mod3, 2,
                   tpb, ROW_TILE)
    x2 = _moe(x2, l0_norm2, mod3, tiles_per_batch, l0_w_router, l0_b_router, l0_w_gate_up,
              l0_b_gate_up, l0_w_down, l0_b_down)

    mod3 = _modulation(c_rows, l1_w_mod, l1_b_mod).reshape(V7X_SUBLANES, 1, 6 * d)
    u = _norm_proj(x2, l1_norm1, mod3, 0, 1, lambda i: i // tpb, l1_w_in.astype(BF16), ROW_TILE)
    x2 = _pool_conv_mixer(u, l1_pool_w, l1_pool_scale, l1_conv_w, l1_w_out.astype(BF16), x2, mod3,
                          2, l, ROW_TILE)
    x2 = _moe(x2, l1_norm2, mod3, tiles_per_batch, l1_w_router, l1_b_router, l1_w_gate_up,
              l1_b_gate_up, l1_w_down, l1_b_down)
    return x2.reshape(b, l, d)
```

```python
import functools

import numpy as np
import jax
import jax.numpy as jnp
from jax import lax
from jax.experimental import pallas as pl
from jax.experimental.pallas import tpu as pltpu

F32 = jnp.float32
BF16 = jnp.bfloat16

V7X_LANES = 128
V7X_SUBLANES = 8
V7X_VMEM_BYTES = 64 * 2**20
VMEM_LIMIT_BYTES = V7X_VMEM_BYTES * 7 // 8

GRID_W = 64
NORM_EPS = 1e-6
ROPE_BASE = 10000.0
MLA_HEADS = 8
MLA_Q_RANK = 512
MLA_KV_RANK = 256
MLA_NOPE = 128
MLA_ROPE = 64
MLA_V = 128
MLA_QK = MLA_NOPE + MLA_ROPE
MLA_QK_PAD = 256
NA_HEADS = 8
NA_DIM = 128
NA_WIN_ROWS = 8
NA_WIN_COLS = 16
NA_Q_ROWS = 8
NA_K_ROWS = 16
POOL_WINDOWS = (2, 4, 8, 16)
POOL_WIDTH = 1024
POOL_GROUP = POOL_WIDTH // len(POOL_WINDOWS)
CONV_CH = 1024
HALO = 16
N_EXPERTS = 32
TOP_K = 4
SWIGLU_LIMIT = 7.0
SWIGLU_ALPHA = 1.702
NEG_BIG = -1e30
LOG2_E = 1.4426950408889634

U0_COLS = 4096
U0_CKV_BLK = 4
U0_NAQ_BLK = 7
U0_NAK_BLK = 15
U0_NAV_BLK = 23

ROW_TILE = 512
MOE_ROWS = 512
MOE_F_TILE = 1024
COMBINE_TOKENS = 256


def _cparams(*sem):
    return pltpu.CompilerParams(dimension_semantics=sem, vmem_limit_bytes=VMEM_LIMIT_BYTES)


def _rms(x, gain):
    ms = jnp.mean(x * x, axis=-1, keepdims=True)
    return x * lax.rsqrt(ms + NORM_EPS) * gain


def _dot_nt(a, b):
    return lax.dot_general(a, b, (((1,), (1,)), ((), ())), preferred_element_type=F32)


def _mod_kernel(c_ref, w_ref, b_ref, o_ref):
    c = c_ref[...]
    s = c * jax.nn.sigmoid(c)
    o_ref[...] = jnp.dot(s, w_ref[...], preferred_element_type=F32,
                         precision=lax.Precision.HIGHEST) + b_ref[...]


def _modulation(c_rows, w_mod, b_mod):
    d, n = w_mod.shape
    tn = 512
    assert n % tn == 0
    return pl.pallas_call(
        _mod_kernel,
        out_shape=jax.ShapeDtypeStruct((V7X_SUBLANES, n), F32),
        grid=(n // tn,),
        in_specs=[pl.BlockSpec((V7X_SUBLANES, d), lambda j: (0, 0)),
                  pl.BlockSpec((d, tn), lambda j: (0, j)),
                  pl.BlockSpec((1, tn), lambda j: (0, j))],
        out_specs=pl.BlockSpec((V7X_SUBLANES, tn), lambda j: (0, j)),
        compiler_params=_cparams("parallel"),
        name="modulation",
    )(c_rows, w_mod, b_mod.reshape(1, n))


def _norm_proj_kernel(x_ref, g_ref, sh_ref, sc_ref, w_ref, o_ref, h_scr):
    @pl.when(pl.program_id(1) == 0)
    def _():
        h = _rms(x_ref[...], g_ref[...]) * (1.0 + sc_ref[...]) + sh_ref[...]
        h_scr[...] = h.astype(BF16)

    o_ref[...] = jnp.dot(h_scr[...], w_ref[...], preferred_element_type=F32).astype(o_ref.dtype)


def _norm_proj(x2, gain, mod3, shift_blk, scale_blk, group_of_tile, w_bf16, tm):
    n, d = x2.shape
    ncol = w_bf16.shape[1]
    tn = 1024
    return pl.pallas_call(
        _norm_proj_kernel,
        out_shape=jax.ShapeDtypeStruct((n, ncol), BF16),
        grid=(n // tm, ncol // tn),
        in_specs=[pl.BlockSpec((tm, d), lambda i, j: (i, 0)),
                  pl.BlockSpec((1, d), lambda i, j: (0, 0)),
                  pl.BlockSpec((None, 1, d), lambda i, j: (group_of_tile(i), 0, shift_blk)),
                  pl.BlockSpec((None, 1, d), lambda i, j: (group_of_tile(i), 0, scale_blk)),
                  pl.BlockSpec((d, tn), lambda i, j: (0, j))],
        out_specs=pl.BlockSpec((tm, tn), lambda i, j: (i, j)),
        scratch_shapes=[pltpu.VMEM((tm, d), BF16)],
        compiler_params=_cparams("parallel", "arbitrary"),
        name="norm_proj",
    )(x2, gain.reshape(1, d), mod3, mod3, w_bf16)


def _lane_iota(shape):
    return lax.broadcasted_iota(jnp.int32, shape, len(shape) - 1)


def _q_prep_kernel(cq_ref, qn_ref, wq_ref, g_ref, cos_ref, sin_ref, o_ref):
    cn = _rms(cq_ref[...].astype(F32), qn_ref[...]).astype(BF16)
    t_all = jnp.dot(cn, wq_ref[...], preferred_element_type=F32)
    in_head = _lane_iota((t_all.shape[0], MLA_QK_PAD)) < MLA_QK
    g = g_ref[...]
    cos = cos_ref[...]
    sin = sin_ref[...]
    for h in range(MLA_HEADS):
        t = t_all[:, h * MLA_QK_PAD:(h + 1) * MLA_QK_PAD]
        sq = jnp.where(in_head, t * t, 0.0)
        r = lax.rsqrt(jnp.sum(sq, axis=-1, keepdims=True) * (1.0 / MLA_QK) + NORM_EPS)
        tn = t * r * g
        rope = tn[:, MLA_NOPE:MLA_QK] * cos + tn[:, MLA_QK:] * sin
        out = jnp.concatenate([tn[:, :MLA_NOPE], rope, jnp.zeros_like(rope)], axis=-1)
        o_ref[h] = (out * (MLA_QK ** -0.5 * LOG2_E)).astype(o_ref.dtype)


def _q_prep(u, b, l, q_norm, wq, gq, cos_t, sin_t, tm):
    nt = l // tm
    return pl.pallas_call(
        _q_prep_kernel,
        out_shape=jax.ShapeDtypeStruct((b, MLA_HEADS, l, MLA_QK_PAD), BF16),
        grid=(b, nt),
        in_specs=[pl.BlockSpec((tm, MLA_Q_RANK), lambda bi, i: (bi * nt + i, 0)),
                  pl.BlockSpec((1, MLA_Q_RANK), lambda bi, i: (0, 0)),
                  pl.BlockSpec((MLA_Q_RANK, MLA_HEADS * MLA_QK_PAD), lambda bi, i: (0, 0)),
                  pl.BlockSpec((1, MLA_QK_PAD), lambda bi, i: (0, 0)),
                  pl.BlockSpec((tm, MLA_ROPE), lambda bi, i: (i, 0)),
                  pl.BlockSpec((tm, MLA_ROPE), lambda bi, i: (i, 0))],
        out_specs=pl.BlockSpec((None, MLA_HEADS, tm, MLA_QK_PAD), lambda bi, i: (bi, 0, i, 0)),
        compiler_params=_cparams("parallel", "parallel"),
        name="mla_q_prep",
    )(u, q_norm.reshape(1, -1), wq, gq, cos_t, sin_t)


def _kv_prep_kernel(ckv_ref, kvn_ref, wkv_ref, g_ref, cos_ref, sin_ref, k_ref, v_ref):
    cn = _rms(ckv_ref[:, :MLA_KV_RANK].astype(F32), kvn_ref[...]).astype(BF16)
    t_all = jnp.dot(cn, wkv_ref[...], preferred_element_type=F32)
    kr = ckv_ref[:, MLA_KV_RANK:].astype(F32)
    kr_ss = jnp.sum(jnp.where(_lane_iota(kr.shape) < MLA_ROPE, kr * kr, 0.0), axis=-1, keepdims=True)
    g = g_ref[...]
    cos = cos_ref[...]
    sin = sin_ref[...]
    head_w = MLA_NOPE + MLA_V
    for h in range(MLA_HEADS):
        kn = t_all[:, h * head_w:h * head_w + MLA_NOPE]
        ss = jnp.sum(kn * kn, axis=-1, keepdims=True) + kr_ss
        r = lax.rsqrt(ss * (1.0 / MLA_QK) + NORM_EPS)
        kn_n = kn * r * g[:, :MLA_NOPE]
        kr_n = kr * r * g[:, MLA_NOPE:]
        rope = kr_n[:, :MLA_ROPE] * cos + kr_n[:, MLA_ROPE:] * sin
        k_ref[h] = jnp.concatenate([kn_n, rope, jnp.zeros_like(rope)], axis=-1).astype(k_ref.dtype)
        v_ref[h] = t_all[:, h * head_w + MLA_NOPE:(h + 1) * head_w].astype(v_ref.dtype)


def _kv_prep(ckv, kv_norm, wkv, gk, cos_t, sin_t, tm):
    b, m, w = ckv.shape
    return pl.pallas_call(
        _kv_prep_kernel,
        out_shape=(jax.ShapeDtypeStruct((b, MLA_HEADS, m, MLA_QK_PAD), BF16),
                   jax.ShapeDtypeStruct((b, MLA_HEADS, m, MLA_V), BF16)),
        grid=(b, m // tm),
        in_specs=[pl.BlockSpec((None, tm, w), lambda bi, i: (bi, i, 0)),
                  pl.BlockSpec((1, MLA_KV_RANK), lambda bi, i: (0, 0)),
                  pl.BlockSpec(wkv.shape, lambda bi, i: (0, 0)),
                  pl.BlockSpec((1, MLA_QK_PAD), lambda bi, i: (0, 0)),
                  pl.BlockSpec((tm, MLA_ROPE), lambda bi, i: (i, 0)),
                  pl.BlockSpec((tm, MLA_ROPE), lambda bi, i: (i, 0))],
        out_specs=(pl.BlockSpec((None, MLA_HEADS, tm, MLA_QK_PAD), lambda bi, i: (bi, 0, i, 0)),
                   pl.BlockSpec((None, MLA_HEADS, tm, MLA_V), lambda bi, i: (bi, 0, i, 0))),
        compiler_params=_cparams("parallel", "parallel"),
        name="mla_kv_prep",
    )(ckv, kv_norm.reshape(1, -1), wkv, gk, cos_t, sin_t)


def _mla_kernel(q_ref, k_ref, v_ref, o_ref, sa_scr, sb_scr, m_scr, l_scr, acc_scr, *, tk, nk):
    m_scr[...] = jnp.full_like(m_scr, NEG_BIG)
    l_scr[...] = jnp.zeros_like(l_scr)
    acc_scr[...] = jnp.zeros_like(acc_scr)

    def scores(c, dst):
        off = pl.multiple_of(c * tk, tk)
        dst[...] = _dot_nt(q_ref[...], k_ref[pl.ds(off, tk), :])

    def consume(c, src):
        off = pl.multiple_of(c * tk, tk)
        s = src[...]
        m_prev = m_scr[...]
        m_new = jnp.maximum(m_prev, jnp.max(s, axis=-1, keepdims=True))
        alpha = jnp.exp2(m_prev - m_new)
        p = jnp.exp2(s - m_new)
        l_scr[...] = alpha * l_scr[...] + jnp.sum(p, axis=-1, keepdims=True)
        acc_scr[...] = alpha * acc_scr[...] + jnp.dot(
            p.astype(BF16), v_ref[pl.ds(off, tk), :], preferred_element_type=F32)
        m_scr[...] = m_new

    scores(0, sa_scr)

    def body(pair, carry):
        c = 2 * pair
        scores(c + 1, sb_scr)
        consume(c, sa_scr)
        scores(c + 2, sa_scr)
        consume(c + 1, sb_scr)
        return carry

    lax.fori_loop(0, (nk - 1) // 2, body, 0)
    if nk % 2 == 1:
        consume(nk - 1, sa_scr)
    else:
        scores(nk - 1, sb_scr)
        consume(nk - 2, sa_scr)
        consume(nk - 1, sb_scr)
    o_ref[...] = (acc_scr[...] / l_scr[...]).astype(o_ref.dtype)


def _mla_attention(q, k, v, tq):
    b, h, l, _ = q.shape
    m = k.shape[2]
    tk = next(t for t in (1280, 1024, 768, 512, 256, 128) if m % t == 0)
    return pl.pallas_call(
        functools.partial(_mla_kernel, tk=tk, nk=m // tk),
        out_shape=jax.ShapeDtypeStruct((b, l, h * MLA_V), BF16),
        grid=(b, h, l // tq),
        in_specs=[pl.BlockSpec((None, None, tq, MLA_QK_PAD), lambda bi, hi, i: (bi, hi, i, 0)),
                  pl.BlockSpec((None, None, m, MLA_QK_PAD), lambda bi, hi, i: (bi, hi, 0, 0)),
                  pl.BlockSpec((None, None, m, MLA_V), lambda bi, hi, i: (bi, hi, 0, 0))],
        out_specs=pl.BlockSpec((None, tq, MLA_V), lambda bi, hi, i: (bi, i, hi)),
        scratch_shapes=[pltpu.VMEM((tq, tk), F32), pltpu.VMEM((tq, tk), F32),
                        pltpu.VMEM((tq, 1), F32), pltpu.VMEM((tq, 1), F32),
                        pltpu.VMEM((tq, MLA_V), F32)],
        compiler_params=_cparams("parallel", "parallel", "arbitrary"),
        name="mla_attention",
    )(q, k, v)


def _na_kernel(q_ref, k_ref, v_ref, kc_ref, vc_ref, gq_ref, gk_ref, bias_ref, o_ref, *, rows):
    rb = pl.program_id(2)
    k_row0 = jnp.clip(rb * NA_Q_ROWS - NA_WIN_ROWS // 2, 0, rows - NA_K_ROWS)
    off = pl.multiple_of(k_row0 * GRID_W, NA_WIN_ROWS // 2 * GRID_W)
    nk = NA_K_ROWS * GRID_W
    q = (_rms(q_ref[...].astype(F32), gq_ref[...]) * (NA_DIM ** -0.5)).astype(BF16)
    kw = _rms(k_ref[pl.ds(off, nk), :].astype(F32), gk_ref[...]).astype(BF16)
    kc = _rms(kc_ref[...].astype(F32), gk_ref[...]).astype(BF16)
    s_w = _dot_nt(q, kw) + bias_ref[...]
    s_c = _dot_nt(q, kc)
    m = jnp.maximum(jnp.max(s_w, axis=-1, keepdims=True), jnp.max(s_c, axis=-1, keepdims=True))
    p_w = jnp.exp(s_w - m)
    p_c = jnp.exp(s_c - m)
    denom = jnp.sum(p_w, axis=-1, keepdims=True) + jnp.sum(p_c, axis=-1, keepdims=True)
    o = (jnp.dot(p_w.astype(BF16), v_ref[pl.ds(off, nk), :], preferred_element_type=F32)
         + jnp.dot(p_c.astype(BF16), vc_ref[...], preferred_element_type=F32))
    o_ref[...] = (o / denom).astype(o_ref.dtype)


def _na_bias_table(rel_bias, rows):
    nrb = rows // NA_Q_ROWS
    tables = []
    for rb in (0, min(1, nrb - 1), nrb - 1):
        k_row0 = int(np.clip(rb * NA_Q_ROWS - NA_WIN_ROWS // 2, 0, rows - NA_K_ROWS))
        r = rb * NA_Q_ROWS + np.arange(NA_Q_ROWS)
        kr = k_row0 + np.arange(NA_K_ROWS)
        qc = np.arange(GRID_W)
        kc = np.arange(GRID_W)
        rs = np.clip(r - NA_WIN_ROWS // 2, 0, rows - NA_WIN_ROWS)
        cs = np.clip(qc - NA_WIN_COLS // 2, 0, GRID_W - NA_WIN_COLS)
        row_ok = (kr[None, :] >= rs[:, None]) & (kr[None, :] < rs[:, None] + NA_WIN_ROWS)
        col_ok = (kc[None, :] >= cs[:, None]) & (kc[None, :] < cs[:, None] + NA_WIN_COLS)
        ri = np.clip(kr[None, :] - r[:, None] + NA_WIN_ROWS - 1, 0, 2 * NA_WIN_ROWS - 2)
        ci = np.clip(kc[None, :] - qc[:, None] + NA_WIN_COLS - 1, 0, 2 * NA_WIN_COLS - 2)
        sel_r = (ri.reshape(-1)[:, None] == np.arange(2 * NA_WIN_ROWS - 1)[None, :]).astype(np.float32)
        sel_c = (np.arange(2 * NA_WIN_COLS - 1)[:, None] == ci.reshape(-1)[None, :]).astype(np.float32)
        bias = jnp.einsum("rm,hmc->hrc", sel_r, rel_bias.astype(F32), precision=lax.Precision.HIGHEST)
        bias = jnp.einsum("hrc,cq->hrq", bias, sel_c, precision=lax.Precision.HIGHEST)
        bias = bias.reshape(NA_HEADS, NA_Q_ROWS, NA_K_ROWS, GRID_W, GRID_W).transpose(0, 1, 3, 2, 4)
        ok = row_ok[:, None, :, None] & col_ok[None, :, None, :]
        bias = jnp.where(ok[None], bias.astype(F32), NEG_BIG)
        tables.append(bias.reshape(NA_HEADS, NA_Q_ROWS * GRID_W, NA_K_ROWS * GRID_W))
    return jnp.stack(tables)


def _na_attention(u3, uctx3, gq, gk, bias):
    b, l, _ = u3.shape
    n_ctx = uctx3.shape[1]
    rows = l // GRID_W
    nrb = rows // NA_Q_ROWS
    tq = NA_Q_ROWS * GRID_W

    def pattern(rb):
        return jnp.where(rb == 0, 0, jnp.where(rb == nrb - 1, 2, 1))

    return pl.pallas_call(
        functools.partial(_na_kernel, rows=rows),
        out_shape=jax.ShapeDtypeStruct((b, l, NA_HEADS * NA_DIM), BF16),
        grid=(b, NA_HEADS, nrb),
        in_specs=[pl.BlockSpec((None, tq, NA_DIM), lambda bi, h, rb: (bi, rb, U0_NAQ_BLK + h)),
                  pl.BlockSpec((None, l, NA_DIM), lambda bi, h, rb: (bi, 0, U0_NAK_BLK + h)),
                  pl.BlockSpec((None, l, NA_DIM), lambda bi, h, rb: (bi, 0, U0_NAV_BLK + h)),
                  pl.BlockSpec((None, n_ctx, NA_DIM), lambda bi, h, rb: (bi, 0, U0_NAK_BLK + h)),
                  pl.BlockSpec((None, n_ctx, NA_DIM), lambda bi, h, rb: (bi, 0, U0_NAV_BLK + h)),
                  pl.BlockSpec((1, NA_DIM), lambda bi, h, rb: (0, 0)),
                  pl.BlockSpec((1, NA_DIM), lambda bi, h, rb: (0, 0)),
                  pl.BlockSpec((None, None, tq, NA_K_ROWS * GRID_W),
                               lambda bi, h, rb: (pattern(rb), h, 0, 0))],
        out_specs=pl.BlockSpec((None, tq, NA_DIM), lambda bi, h, rb: (bi, rb, h)),
        compiler_params=_cparams("parallel", "parallel", "arbitrary"),
        name="neighbourhood_attention",
    )(u3, u3, u3, uctx3, uctx3, gq.reshape(1, -1), gk.reshape(1, -1), bias)


def _attn_out_kernel(am_ref, an_ref, wm_ref, wn_ref, x_ref, gate_ref, o_ref):
    y = (jnp.dot(am_ref[...], wm_ref[...], preferred_element_type=F32)
         + jnp.dot(an_ref[...], wn_ref[...], preferred_element_type=F32))
    o_ref[...] = x_ref[...] + gate_ref[...] * y


def _attn_out(a_m, a_n, w_out_bf16, x2, mod3, gate_blk, tiles_per_batch, tm):
    n, d = x2.shape
    half = a_m.shape[1]
    return pl.pallas_call(
        _attn_out_kernel,
        out_shape=jax.ShapeDtypeStruct((n, d), F32),
        grid=(n // tm,),
        in_specs=[pl.BlockSpec((tm, half), lambda i: (i, 0)),
                  pl.BlockSpec((tm, half), lambda i: (i, 0)),
                  pl.BlockSpec((half, d), lambda i: (0, 0)),
                  pl.BlockSpec((half, d), lambda i: (1, 0)),
                  pl.BlockSpec((tm, d), lambda i: (i, 0)),
                  pl.BlockSpec((None, 1, d), lambda i: (i // tiles_per_batch, 0, gate_blk))],
        out_specs=pl.BlockSpec((tm, d), lambda i: (i, 0)),
        compiler_params=_cparams("parallel"),
        name="attn_out_residual",
    )(a_m, a_n, w_out_bf16, w_out_bf16, x2, mod3)


def _pool_conv_kernel(prev_ref, u_ref, next_ref, pw_ref, ps_ref, cw_ref, wp_ref, wc_ref, x_ref,
                      gate_ref, o_ref, *, seq_len, tiles_per_batch):
    tm = u_ref.shape[0]
    n_ext = tm + 2 * HALO
    pos0 = (pl.program_id(0) % tiles_per_batch) * tm
    pos_ext = pos0 - HALO + lax.broadcasted_iota(jnp.int32, (n_ext, 1), 0)
    valid = (pos_ext >= 0) & (pos_ext < seq_len)

    def ext(c0, c1):
        e = jnp.concatenate([prev_ref[:, c0:c1], u_ref[:, c0:c1], next_ref[:, c0:c1]], axis=0)
        return jnp.where(valid, e.astype(F32), 0.0)

    def shifted(a, d):
        return pltpu.roll(a, (-d) % n_ext, axis=0)

    pos = (pos0 + lax.broadcasted_iota(jnp.int32, (tm, 1), 0)).astype(F32)
    y = jnp.zeros((tm, x_ref.shape[1]), F32)
    for g, w in enumerate(POOL_WINDOWS):
        c0 = g * POOL_GROUP
        e = ext(c0, c0 + POOL_GROUP)
        acc = e + shifted(e, -1)
        span = 1
        while 2 * span < w:
            acc = shifted(acc, span) + shifted(acc, -span)
            span *= 2
        half = w // 2
        cnt = jnp.minimum(pos + half, float(seq_len)) - jnp.maximum(pos - half, 0.0)
        pooled = acc[HALO:HALO + tm] / cnt - e[HALO:HALO + tm]
        mixed = jnp.dot(pooled.astype(BF16), pw_ref[g], preferred_element_type=F32)
        y_g = (mixed * ps_ref[:, c0:c0 + POOL_GROUP]).astype(BF16)
        y = y + jnp.dot(y_g, wp_ref[c0:c0 + POOL_GROUP, :], preferred_element_type=F32)

    gb0 = POOL_WIDTH
    gc0 = POOL_WIDTH + CONV_CH
    v0 = POOL_WIDTH + 2 * CONV_CH
    z = ext(gc0, gc0 + CONV_CH) * ext(v0, v0 + CONV_CH)
    cw = cw_ref[...]
    conv = cw[0:1] * shifted(z, -1) + cw[1:2] * z + cw[2:3] * shifted(z, 1)
    y_conv = u_ref[:, gb0:gb0 + CONV_CH].astype(F32) * conv[HALO:HALO + tm]
    y = y + jnp.dot(y_conv.astype(BF16), wc_ref[...], preferred_element_type=F32)
    o_ref[...] = x_ref[...] + gate_ref[...] * y


def _pool_conv_mixer(u, pool_w, pool_scale, conv_w, w_out_bf16, x2, mod3, gate_blk, seq_len, tm):
    n, d = x2.shape
    width = u.shape[1]
    tiles_per_batch = seq_len // tm
    hb = tm // HALO
    last = n // HALO - 1
    return pl.pallas_call(
        functools.partial(_pool_conv_kernel, seq_len=seq_len, tiles_per_batch=tiles_per_batch),
        out_shape=jax.ShapeDtypeStruct((n, d), F32),
        grid=(n // tm,),
        in_specs=[pl.BlockSpec((HALO, width), lambda i: (jnp.maximum(i * hb - 1, 0), 0)),
                  pl.BlockSpec((tm, width), lambda i: (i, 0)),
                  pl.BlockSpec((HALO, width), lambda i: (jnp.minimum((i + 1) * hb, last), 0)),
                  pl.BlockSpec(pool_w.shape, lambda i: (0, 0, 0)),
                  pl.BlockSpec((1, POOL_WIDTH), lambda i: (0, 0)),
                  pl.BlockSpec((V7X_SUBLANES, CONV_CH), lambda i: (0, 0)),
                  pl.BlockSpec((POOL_WIDTH, d), lambda i: (0, 0)),
                  pl.BlockSpec((CONV_CH, d), lambda i: (1, 0)),
                  pl.BlockSpec((tm, d), lambda i: (i, 0)),
                  pl.BlockSpec((None, 1, d), lambda i: (i // tiles_per_batch, 0, gate_blk))],
        out_specs=pl.BlockSpec((tm, d), lambda i: (i, 0)),
        compiler_params=_cparams("parallel"),
        name="pool_conv_mixer",
    )(u, u, u, pool_w.astype(BF16), pool_scale.reshape(1, -1),
      jnp.pad(conv_w, ((0, V7X_SUBLANES - conv_w.shape[0]), (0, 0))), w_out_bf16, w_out_bf16, x2, mod3)


def _split_bf16(a):
    hi = a.astype(BF16)
    return hi, (a - hi.astype(F32)).astype(BF16)


def _pack_bf16_pairs(lo, hi):
    lo_bits = lax.bitcast_convert_type(lo.astype(BF16).astype(F32), jnp.uint32) >> 16
    hi_bits = lax.bitcast_convert_type(hi.astype(BF16).astype(F32), jnp.uint32) & jnp.uint32(0xFFFF0000)
    return hi_bits | lo_bits


def _unpack_pairs_f32(w):
    lo = lax.bitcast_convert_type(w << 16, F32)
    hi = lax.bitcast_convert_type(w & jnp.uint32(0xFFFF0000), F32)
    return lo, hi


def _unpack_bf16_pairs(w):
    lo, hi = _unpack_pairs_f32(w)
    return lo.astype(BF16), hi.astype(BF16)


def _router_kernel(x_ref, g_ref, sh_ref, sc_ref, wr_ref, br_ref, tri_ref, hp_ref, idx_ref, gate_ref,
                   rank_ref, cnt_ref, run_scr, *, p_rows):
    tm, d = x_ref.shape

    @pl.when(pl.program_id(0) == 0)
    def _():
        run_scr[...] = jnp.zeros_like(run_scr)

    h = _rms(x_ref[...], g_ref[...]) * (1.0 + sc_ref[...]) + sh_ref[...]
    packed = _pack_bf16_pairs(h[:, :d // 2], h[:, d // 2:])
    for s in range(p_rows):
        hp_ref[pl.ds(s, tm, stride=p_rows), :] = packed[:, s * V7X_LANES:(s + 1) * V7X_LANES]
    h_hi, h_lo = _split_bf16(h)
    w_hi, w_lo = _split_bf16(wr_ref[...])
    logits = _dot_nt(w_hi, h_hi) + _dot_nt(w_hi, h_lo) + _dot_nt(w_lo, h_hi) + br_ref[...]
    expert = lax.broadcasted_iota(jnp.int32, logits.shape, 0)
    vals, ids = [], []
    for _ in range(TOP_K):
        mx = jnp.max(logits, axis=0, keepdims=True)
        am = jnp.min(jnp.where(logits == mx, expert, N_EXPERTS), axis=0, keepdims=True)
        vals.append(mx)
        ids.append(am)
        logits = jnp.where(expert == am, -jnp.inf, logits)
    ex = [jnp.exp(v - vals[0]) for v in vals]
    tot = ex[0] + ex[1] + ex[2] + ex[3]
    idx_ref[...] = jnp.concatenate(ids, axis=0)
    gate_ref[...] = jnp.concatenate([e / tot for e in ex], axis=0)

    running = run_scr[...]
    ranks = []
    for k in range(TOP_K):
        hit = expert == ids[k]
        onehot = jnp.where(hit, 1.0, 0.0)
        before = jnp.dot(onehot.astype(BF16), tri_ref[...], preferred_element_type=F32)
        rank = jnp.sum(jnp.where(hit, before + running, 0.0), axis=0, keepdims=True)
        ranks.append(rank.astype(jnp.int32))
        running = running + jnp.sum(onehot, axis=1, keepdims=True)
    run_scr[...] = running
    rank_ref[...] = jnp.concatenate(ranks, axis=0)
    cnt_ref[...] = jnp.broadcast_to(running, cnt_ref.shape).astype(jnp.int32)


def _router(x2, gain, mod3, shift_blk, scale_blk, tiles_per_batch, w_router, b_router, tm):
    n, d = x2.shape
    p_rows = d // 2 // V7X_LANES
    slot = jax.ShapeDtypeStruct((TOP_K, n), jnp.int32)
    slot_spec = pl.BlockSpec((TOP_K, tm), lambda i: (0, i))
    tri = jnp.triu(jnp.ones((tm, tm), BF16), k=1)
    return pl.pallas_call(
        functools.partial(_router_kernel, p_rows=p_rows),
        out_shape=(jax.ShapeDtypeStruct((n * p_rows, V7X_LANES), jnp.uint32),
                   slot, jax.ShapeDtypeStruct((TOP_K, n), F32), slot,
                   jax.ShapeDtypeStruct((N_EXPERTS, V7X_LANES), jnp.int32)),
        grid=(n // tm,),
        in_specs=[pl.BlockSpec((tm, d), lambda i: (i, 0)),
                  pl.BlockSpec((1, d), lambda i: (0, 0)),
                  pl.BlockSpec((None, 1, d), lambda i: (i // tiles_per_batch, 0, shift_blk)),
                  pl.BlockSpec((None, 1, d), lambda i: (i // tiles_per_batch, 0, scale_blk)),
                  pl.BlockSpec((N_EXPERTS, d), lambda i: (0, 0)),
                  pl.BlockSpec((N_EXPERTS, 1), lambda i: (0, 0)),
                  pl.BlockSpec((tm, tm), lambda i: (0, 0))],
        out_specs=(pl.BlockSpec((tm * p_rows, V7X_LANES), lambda i: (i, 0)),
                   slot_spec, slot_spec, slot_spec,
                   pl.BlockSpec((N_EXPERTS, V7X_LANES), lambda i: (0, 0))),
        scratch_shapes=[pltpu.VMEM((N_EXPERTS, 1), F32)],
        compiler_params=_cparams("arbitrary"),
        name="moe_router",
    )(x2, gain.reshape(1, d), mod3, mod3, w_router.T, b_router.reshape(-1, 1), tri)


def _slot_pos_kernel(pstart_ref, idx_ref, rank_ref, pos_ref):
    idx = idx_ref[...]
    pos = rank_ref[...]
    for e in range(N_EXPERTS):
        pos = pos + jnp.where(idx == e, pstart_ref[e], 0)
    pos_ref[...] = pos


def _slot_positions(pstart, idx_t, rank_t):
    k, n = idx_t.shape
    tn = next(t for t in (4096, 2048, 1024, 512) if n % t == 0)
    spec = pl.BlockSpec((k, tn), lambda i, ps: (0, i))
    return pl.pallas_call(
        _slot_pos_kernel,
        out_shape=jax.ShapeDtypeStruct((k, n), jnp.int32),
        grid_spec=pltpu.PrefetchScalarGridSpec(
            num_scalar_prefetch=1, grid=(n // tn,), in_specs=[spec, spec], out_specs=spec),
        compiler_params=_cparams("parallel"),
        name="moe_slot_positions",
    )(pstart, idx_t, rank_t)


def _row_dma_burst(idx_hbm_row, idx_smem, idx_sem, row_copy):
    n = idx_smem.shape[0]
    cp = pltpu.make_async_copy(idx_hbm_row, idx_smem, idx_sem)
    cp.start()
    cp.wait()

    def issue(pair, c):
        r = 2 * pair
        row_copy(r, idx_smem[r]).start(priority=0)
        row_copy(r + 1, idx_smem[r + 1]).start(priority=1)
        return c

    def drain(r, c):
        row_copy(r, 0).wait()
        return c

    lax.fori_loop(0, n // 2, issue, 0, unroll=4)
    lax.fori_loop(0, n, drain, 0, unroll=8)


def _dispatch_kernel(pos_hbm, hp_ref, xs_zero_hbm, xs_hbm, idx_smem, sem, idx_sem, *, p_rows):
    del xs_zero_hbm
    tt = hp_ref.shape[0] // p_rows

    def row_copy(r, dst_row):
        t = r & (tt - 1)
        src = hp_ref.at[pl.ds(pl.multiple_of(t * p_rows, p_rows), p_rows)]
        return pltpu.make_async_copy(src, xs_hbm.at[dst_row], sem)

    _row_dma_burst(pos_hbm.at[pl.program_id(0)], idx_smem, idx_sem, row_copy)


def _dispatch(pos_tiles, h_packed, n_rows, p_rows):
    n_tiles, slots = pos_tiles.shape
    tt = slots // TOP_K
    assert tt & (tt - 1) == 0
    xs_shape = jax.ShapeDtypeStruct((n_rows, p_rows, V7X_LANES), jnp.uint32)
    return pl.pallas_call(
        functools.partial(_dispatch_kernel, p_rows=p_rows),
        out_shape=xs_shape,
        grid=(n_tiles,),
        in_specs=[pl.BlockSpec(memory_space=pl.ANY),
                  pl.BlockSpec((tt * p_rows, V7X_LANES), lambda i: (i, 0)),
                  pl.BlockSpec(memory_space=pl.ANY)],
        out_specs=pl.BlockSpec(memory_space=pl.ANY),
        scratch_shapes=[pltpu.SMEM((slots,), jnp.int32),
                        pltpu.SemaphoreType.DMA,
                        pltpu.SemaphoreType.DMA],
        input_output_aliases={2: 0},
        compiler_params=_cparams("arbitrary"),
        name="moe_dispatch",
    )(pos_tiles, h_packed, jnp.zeros(xs_shape.shape, xs_shape.dtype))


def _experts_kernel(be_ref, nused_ref, xs_ref, wg_ref, wu_ref, bg_ref, bu_ref, wd_ref, bd_ref,
                    y_ref, x_scr, acc_scr, *, p_rows):
    i = pl.program_id(0)
    j = pl.program_id(1)
    tb, d = x_scr.shape
    active = i < nused_ref[0]

    @pl.when(active & (j == 0))
    def _():
        for s in range(p_rows):
            lo, hi = _unpack_bf16_pairs(xs_ref[pl.ds(s, tb, stride=p_rows), :])
            x_scr[:, s * V7X_LANES:(s + 1) * V7X_LANES] = lo
            x_scr[:, d // 2 + s * V7X_LANES:d // 2 + (s + 1) * V7X_LANES] = hi
        acc_scr[...] = jnp.zeros_like(acc_scr)

    @pl.when(active)
    def _():
        x = x_scr[...]
        a_gate = jnp.dot(x, wg_ref[...], preferred_element_type=F32) + bg_ref[...]
        a_up = jnp.dot(x, wu_ref[...], preferred_element_type=F32) + bu_ref[...]
        a_gate = jnp.minimum(a_gate, SWIGLU_LIMIT)
        a_up = jnp.clip(a_up, -SWIGLU_LIMIT, SWIGLU_LIMIT)
        act = (a_up + 1.0) * (a_gate * jax.nn.sigmoid(SWIGLU_ALPHA * a_gate))
        acc_scr[...] += jnp.dot(act.astype(BF16), wd_ref[...], preferred_element_type=F32)

    last = j == pl.num_programs(1) - 1

    @pl.when(active & last)
    def _():
        y = acc_scr[...] + bd_ref[...]
        packed = _pack_bf16_pairs(y[:, :d // 2], y[:, d // 2:])
        for s in range(p_rows):
            y_ref[pl.ds(s, tb, stride=p_rows), :] = packed[:, s * V7X_LANES:(s + 1) * V7X_LANES]

    @pl.when(jnp.logical_not(active) & last)
    def _():
        y_ref[...] = jnp.zeros_like(y_ref)


def _experts(block_expert, n_used, xs_rows, tb, w_gate_up, b_gate_up, w_down, b_down):
    e, d, two_f = w_gate_up.shape
    p_rows = d // 2 // V7X_LANES
    n_blocks = xs_rows.shape[0] // (tb * p_rows)
    f = two_f // 2
    tf = MOE_F_TILE
    nf = f // tf
    row_spec = pl.BlockSpec((tb * p_rows, V7X_LANES), lambda i, j, be, nu: (i, 0))

    def fcol(i, j, nused):
        return jnp.where(i < nused[0], j, nf - 1)

    grid_spec = pltpu.PrefetchScalarGridSpec(
        num_scalar_prefetch=2,
        grid=(n_blocks, nf),
        in_specs=[row_spec,
                  pl.BlockSpec((None, d, tf), lambda i, j, be, nu: (be[i], 0, fcol(i, j, nu))),
                  pl.BlockSpec((None, d, tf), lambda i, j, be, nu: (be[i], 0, nf + fcol(i, j, nu))),
                  pl.BlockSpec((None, 1, tf), lambda i, j, be, nu: (be[i], 0, fcol(i, j, nu))),
                  pl.BlockSpec((None, 1, tf), lambda i, j, be, nu: (be[i], 0, nf + fcol(i, j, nu))),
                  pl.BlockSpec((None, tf, d), lambda i, j, be, nu: (be[i], fcol(i, j, nu), 0)),
                  pl.BlockSpec((None, 1, d), lambda i, j, be, nu: (be[i], 0, 0))],
        out_specs=row_spec,
        scratch_shapes=[pltpu.VMEM((tb, d), BF16),
                        pltpu.VMEM((tb, d), F32)])
    return pl.pallas_call(
        functools.partial(_experts_kernel, p_rows=p_rows),
        out_shape=jax.ShapeDtypeStruct(xs_rows.shape, jnp.uint32),
        grid_spec=grid_spec,
        compiler_params=_cparams("arbitrary", "arbitrary"),
        name="moe_experts",
    )(block_expert, n_used, xs_rows, w_gate_up, w_gate_up,
      b_gate_up.reshape(e, 1, two_f), b_gate_up.reshape(e, 1, two_f), w_down,
      b_down.reshape(e, 1, d))


def _combine_kernel(pos_hbm, y_hbm, x_ref, mgate_ref, rgate_ref, o_ref, idx_smem, gbuf, sem,
                    idx_sem, *, p_rows):
    tt, d = x_ref.shape

    def row_copy(r, src_row):
        dst = gbuf.at[pl.ds(pl.multiple_of(r * p_rows, p_rows), p_rows)]
        return pltpu.make_async_copy(y_hbm.at[src_row], dst, sem)

    _row_dma_burst(pos_hbm.at[pl.program_id(0)], idx_smem, idx_sem, row_copy)
    rg = rgate_ref[...]
    gates = [jnp.broadcast_to(rg[:, k:k + 1], (tt, V7X_LANES)) for k in range(TOP_K)]
    for s in range(p_rows):
        lo_tot = jnp.zeros((tt, V7X_LANES), F32)
        hi_tot = jnp.zeros((tt, V7X_LANES), F32)
        for k in range(TOP_K):
            lo, hi = _unpack_pairs_f32(gbuf[pl.ds(k * tt * p_rows + s, tt, stride=p_rows), :])
            lo_tot = lo_tot + gates[k] * lo
            hi_tot = hi_tot + gates[k] * hi
        for base, tot in ((0, lo_tot), (d // 2, hi_tot)):
            cols = slice(base + s * V7X_LANES, base + (s + 1) * V7X_LANES)
            o_ref[:, cols] = x_ref[:, cols] + mgate_ref[:, cols] * tot


def _combine(pos_tiles, y_rows, x2, mod3, gate_blk, router_gates, tiles_per_batch, tt):
    n, d = x2.shape
    p_rows = y_rows.shape[1]
    return pl.pallas_call(
        functools.partial(_combine_kernel, p_rows=p_rows),
        out_shape=jax.ShapeDtypeStruct((n, d), F32),
        grid=(n // tt,),
        in_specs=[pl.BlockSpec(memory_space=pl.ANY),
                  pl.BlockSpec(memory_space=pl.ANY),
                  pl.BlockSpec((tt, d), lambda i: (i, 0)),
                  pl.BlockSpec((None, 1, d), lambda i: (i // tiles_per_batch, 0, gate_blk)),
                  pl.BlockSpec((tt, TOP_K), lambda i: (i, 0))],
        out_specs=pl.BlockSpec((tt, d), lambda i: (i, 0)),
        scratch_shapes=[pltpu.SMEM((TOP_K * tt,), jnp.int32),
                        pltpu.VMEM((TOP_K * tt * p_rows, V7X_LANES), jnp.uint32),
                        pltpu.SemaphoreType.DMA,
                        pltpu.SemaphoreType.DMA],
        compiler_params=_cparams("arbitrary"),
        name="moe_combine",
    )(pos_tiles, y_rows, x2, mod3, router_gates)


def _block_tables(counts, n_slots, tb):
    padded = (counts + tb - 1) // tb * tb
    pend = jnp.cumsum(padded)
    pstart = (pend - padded).astype(jnp.int32)
    n_blocks = -(-n_slots // tb) + N_EXPERTS
    block_expert = jnp.minimum(
        jnp.searchsorted(pend, jnp.arange(n_blocks, dtype=jnp.int32) * tb, side="right"),
        N_EXPERTS - 1).astype(jnp.int32)
    n_used = (pend[-1] // tb).astype(jnp.int32).reshape(1)
    return pstart, block_expert, n_used, n_blocks


def _moe(x2, norm2, mod3, tiles_per_batch_fn, w_router, b_router, w_gate_up, b_gate_up, w_down,
         b_down):
    n, d = x2.shape
    p_rows = d // 2 // V7X_LANES
    h_packed, idx_t, gate_t, rank_t, counts = _router(
        x2, norm2, mod3, 3, 4, tiles_per_batch_fn(ROW_TILE), w_router, b_router, ROW_TILE)
    pstart, block_expert, n_used, n_blocks = _block_tables(counts[:, 0], TOP_K * n, MOE_ROWS)
    pos = _slot_positions(pstart, idx_t, rank_t)
    tt = COMBINE_TOKENS
    pos_tiles = pos.reshape(TOP_K, n // tt, tt).transpose(1, 0, 2).reshape(n // tt, TOP_K * tt)
    xs = _dispatch(pos_tiles, h_packed, n_blocks * MOE_ROWS, p_rows)
    y_rows = _experts(block_expert, n_used, xs.reshape(-1, V7X_LANES), MOE_ROWS,
                      w_gate_up.astype(BF16), b_gate_up, w_down.astype(BF16), b_down)
    return _combine(pos_tiles, y_rows.reshape(-1, p_rows, V7X_LANES), x2, mod3, 5, gate_t.T,
                    tiles_per_batch_fn(tt), tt)


def _rope_swap_perm():
    half = MLA_ROPE // 2
    quarter = half // 2
    j = np.arange(MLA_ROPE)
    return (j // half) * half + (j % half + quarter) % half


def _rope_tables(n_tokens):
    t = jnp.arange(n_tokens, dtype=jnp.int32)
    row = (t // GRID_W).astype(F32)
    col = (t % GRID_W).astype(F32)
    n_freq = MLA_ROPE // 4
    inv_freq = ROPE_BASE ** (-jnp.arange(n_freq, dtype=F32) / n_freq)
    ar = row[:, None] * inv_freq
    ac = col[:, None] * inv_freq
    cos_t = jnp.concatenate([jnp.cos(ar), jnp.cos(ar), jnp.cos(ac), jnp.cos(ac)], axis=-1)
    sin_t = jnp.concatenate([-jnp.sin(ar), jnp.sin(ar), -jnp.sin(ac), jnp.sin(ac)], axis=-1)
    return cos_t, sin_t


def kernel(x, c, ctx, c_ctx, l0_w_mod, l0_b_mod, l0_norm1, l0_w_in, l0_mla_q_norm, l0_mla_w_q_up, l0_mla_kv_norm, l0_mla_w_kv_up, l0_mla_qk_q, l0_mla_qk_k, l0_na_qk_q, l0_na_qk_k, l0_na_rel_bias, l0_w_out, l0_norm2, l0_w_router, l0_b_router, l0_w_gate_up, l0_b_gate_up, l0_w_down, l0_b_down, l1_w_mod, l1_b_mod, l1_norm1, l1_w_in, l1_pool_w, l1_pool_scale, l1_conv_w, l1_w_out, l1_norm2, l1_w_router, l1_b_router, l1_w_gate_up, l1_b_gate_up, l1_w_down, l1_b_down):
    b, l, d = x.shape
    n_ctx = ctx.shape[1]
    n = b * l
    assert d % V7X_LANES == 0 and l % ROW_TILE == 0 and l % (NA_Q_ROWS * GRID_W) == 0
    assert l // GRID_W >= NA_K_ROWS and b + 1 <= V7X_SUBLANES
    x2 = x.reshape(n, d)

    def tiles_per_batch(tm):
        return l // tm

    c_rows = jnp.concatenate(
        [c, c_ctx[None], jnp.zeros((V7X_SUBLANES - b - 1, d), F32)], axis=0)
    perm = _rope_swap_perm()

    mod3 = _modulation(c_rows, l0_w_mod, l0_b_mod).reshape(V7X_SUBLANES, 1, 6 * d)
    kr0 = MLA_Q_RANK + MLA_KV_RANK
    kr1 = kr0 + MLA_ROPE
    w_in = jnp.concatenate(
        [l0_w_in[:, :kr1], l0_w_in[:, kr0:kr1][:, perm], l0_w_in[:, kr1:],
         jnp.zeros((d, U0_COLS - l0_w_in.shape[1] - MLA_ROPE), F32)], axis=1).astype(BF16)
    tpb = tiles_per_batch(ROW_TILE)
    u = _norm_proj(x2, l0_norm1, mod3, 0, 1, lambda i: i // tpb, w_in, ROW_TILE)
    ctx_tile = n_ctx if (b * n_ctx) % ROW_TILE else ROW_TILE
    u_ctx = _norm_proj(ctx.reshape(b * n_ctx, d), l0_norm1, mod3, 0, 1, lambda i: b, w_in, ctx_tile)

    wq = l0_mla_w_q_up.reshape(MLA_Q_RANK, MLA_HEADS, MLA_QK)
    wq = jnp.concatenate([wq, wq[:, :, MLA_NOPE:][:, :, perm]], axis=-1)
    wq = wq.reshape(MLA_Q_RANK, -1).astype(BF16)
    gq = jnp.concatenate([l0_mla_qk_q, l0_mla_qk_q[MLA_NOPE:][perm]]).reshape(1, -1)
    gk = jnp.concatenate([l0_mla_qk_k, l0_mla_qk_k[MLA_NOPE:][perm]]).reshape(1, -1)
    wkv = l0_mla_w_kv_up.astype(BF16)
    cos_t, sin_t = _rope_tables(l)
    cos_k = jnp.concatenate([cos_t, jnp.ones((n_ctx, MLA_ROPE), F32)], axis=0)
    sin_k = jnp.concatenate([sin_t, jnp.zeros((n_ctx, MLA_ROPE), F32)], axis=0)
    q = _q_prep(u, b, l, l0_mla_q_norm, wq, gq, cos_t, sin_t, ROW_TILE)
    u3 = u.reshape(b, l, U0_COLS)
    uctx3 = u_ctx.reshape(b, n_ctx, U0_COLS)
    ckv0 = U0_CKV_BLK * V7X_LANES
    ckv1 = ckv0 + MLA_KV_RANK + 2 * MLA_ROPE
    ckv = jnp.concatenate([u3[:, :, ckv0:ckv1], uctx3[:, :, ckv0:ckv1]], axis=1)
    k_m, v_m = _kv_prep(ckv, l0_mla_kv_norm, wkv, gk, cos_k, sin_k, n_ctx)
    o_m = _mla_attention(q, k_m, v_m, ROW_TILE)

    bias = _na_bias_table(l0_na_rel_bias, l // GRID_W)
    o_n = _na_attention(u3, uctx3, l0_na_qk_q, l0_na_qk_k, bias)

    x2 = _attn_out(o_m.reshape(n, -1), o_n.reshape(n, -1), l0_w_out.astype(BF16), x2, mod3, 2,
                   tpb, ROW_TILE)
    x2 = _moe(x2, l0_norm2, mod3, tiles_per_batch, l0_w_router, l0_b_router, l0_w_gate_up,
              l0_b_gate_up, l0_w_down, l0_b_down)

    mod3 = _modulation(c_rows, l1_w_mod, l1_b_mod).reshape(V7X_SUBLANES, 1, 6 * d)
    u = _norm_proj(x2, l1_norm1, mod3, 0, 1, lambda i: i // tpb, l1_w_in.astype(BF16), ROW_TILE)
    x2 = _pool_conv_mixer(u, l1_pool_w, l1_pool_scale, l1_conv_w, l1_w_out.astype(BF16), x2, mod3,
                          2, l, ROW_TILE)
    x2 = _moe(x2, l1_norm2, mod3, tiles_per_batch, l1_w_router, l1_b_router, l1_w_gate_up,
              l1_b_gate_up, l1_w_down, l1_b_down)
    return x2.reshape(b, l, d)
```

```python
import functools

import numpy as np
import jax
import jax.numpy as jnp
from jax import lax
from jax.experimental import pallas as pl
from jax.experimental.pallas import tpu as pltpu

F32 = jnp.float32
BF16 = jnp.bfloat16

V7X_LANES = 128
V7X_SUBLANES = 8
V7X_VMEM_BYTES = 64 * 2**20
VMEM_LIMIT_BYTES = V7X_VMEM_BYTES * 7 // 8

GRID_W = 64
NORM_EPS = 1e-6
ROPE_BASE = 10000.0
MLA_HEADS = 8
MLA_Q_RANK = 512
MLA_KV_RANK = 256
MLA_NOPE = 128
MLA_ROPE = 64
MLA_V = 128
MLA_QK = MLA_NOPE + MLA_ROPE
MLA_QK_PAD = 256
MLA_PV_TILE = 256
NA_HEADS = 8
NA_DIM = 128
NA_WIN_ROWS = 8
NA_WIN_COLS = 16
NA_Q_ROWS = 8
NA_K_ROWS = 16
POOL_WINDOWS = (2, 4, 8, 16)
POOL_WIDTH = 1024
POOL_GROUP = POOL_WIDTH // len(POOL_WINDOWS)
CONV_CH = 1024
HALO = 16
N_EXPERTS = 32
TOP_K = 4
SWIGLU_LIMIT = 7.0
SWIGLU_ALPHA = 1.702
NEG_BIG = -1e30
LOG2_E = 1.4426950408889634

U0_COLS = 4096
U0_CKV_BLK = 4
U0_NAQ_BLK = 7
U0_NAK_BLK = 15
U0_NAV_BLK = 23

ROW_TILE = 512
MOE_ROWS = 512
MOE_F_TILE = 1024
COMBINE_TOKENS = 256


def _cparams(*sem):
    return pltpu.CompilerParams(dimension_semantics=sem, vmem_limit_bytes=VMEM_LIMIT_BYTES)


def _rms(x, gain):
    ms = jnp.mean(x * x, axis=-1, keepdims=True)
    return x * lax.rsqrt(ms + NORM_EPS) * gain


def _dot_nt(a, b):
    return lax.dot_general(a, b, (((1,), (1,)), ((), ())), preferred_element_type=F32)


def _mod_kernel(c_ref, w_ref, b_ref, o_ref):
    c = c_ref[...]
    s = c * jax.nn.sigmoid(c)
    o_ref[...] = jnp.dot(s, w_ref[...], preferred_element_type=F32,
                         precision=lax.Precision.HIGHEST) + b_ref[...]


def _modulation(c_rows, w_mod, b_mod):
    d, n = w_mod.shape
    tn = 512
    assert n % tn == 0
    return pl.pallas_call(
        _mod_kernel,
        out_shape=jax.ShapeDtypeStruct((V7X_SUBLANES, n), F32),
        grid=(n // tn,),
        in_specs=[pl.BlockSpec((V7X_SUBLANES, d), lambda j: (0, 0)),
                  pl.BlockSpec((d, tn), lambda j: (0, j)),
                  pl.BlockSpec((1, tn), lambda j: (0, j))],
        out_specs=pl.BlockSpec((V7X_SUBLANES, tn), lambda j: (0, j)),
        compiler_params=_cparams("parallel"),
        name="modulation",
    )(c_rows, w_mod, b_mod.reshape(1, n))


def _norm_proj_kernel(x_ref, g_ref, sh_ref, sc_ref, w_ref, o_ref, *, tn):
    h = (_rms(x_ref[...], g_ref[...]) * (1.0 + sc_ref[...]) + sh_ref[...]).astype(BF16)
    for c in range(0, w_ref.shape[1], tn):
        o_ref[:, c:c + tn] = jnp.dot(
            h, w_ref[:, c:c + tn], preferred_element_type=F32).astype(o_ref.dtype)


def _norm_proj(x2, gain, mod3, shift_blk, scale_blk, group_of_tile, w_bf16, tm):
    n, d = x2.shape
    ncol = w_bf16.shape[1]
    return pl.pallas_call(
        functools.partial(_norm_proj_kernel, tn=1024),
        out_shape=jax.ShapeDtypeStruct((n, ncol), BF16),
        grid=(n // tm,),
        in_specs=[pl.BlockSpec((tm, d), lambda i: (i, 0)),
                  pl.BlockSpec((1, d), lambda i: (0, 0)),
                  pl.BlockSpec((None, 1, d), lambda i: (group_of_tile(i), 0, shift_blk)),
                  pl.BlockSpec((None, 1, d), lambda i: (group_of_tile(i), 0, scale_blk)),
                  pl.BlockSpec((d, ncol), lambda i: (0, 0), pipeline_mode=pl.Buffered(1))],
        out_specs=pl.BlockSpec((tm, ncol), lambda i: (i, 0)),
        compiler_params=_cparams("parallel"),
        name="norm_proj",
    )(x2, gain.reshape(1, d), mod3, mod3, w_bf16)


def _lane_iota(shape):
    return lax.broadcasted_iota(jnp.int32, shape, len(shape) - 1)


def _q_prep_kernel(cq_ref, qn_ref, wq_ref, g_ref, cos_ref, sin_ref, o_ref):
    cn = _rms(cq_ref[...].astype(F32), qn_ref[...]).astype(BF16)
    t_all = jnp.dot(cn, wq_ref[...], preferred_element_type=F32)
    in_head = _lane_iota((t_all.shape[0], MLA_QK_PAD)) < MLA_QK
    g = g_ref[...]
    cos = cos_ref[...]
    sin = sin_ref[...]
    for h in range(MLA_HEADS):
        t = t_all[:, h * MLA_QK_PAD:(h + 1) * MLA_QK_PAD]
        sq = jnp.where(in_head, t * t, 0.0)
        r = lax.rsqrt(jnp.sum(sq, axis=-1, keepdims=True) * (1.0 / MLA_QK) + NORM_EPS)
        tn = t * r * g
        rope = tn[:, MLA_NOPE:MLA_QK] * cos + tn[:, MLA_QK:] * sin
        out = jnp.concatenate([tn[:, :MLA_NOPE], rope, jnp.zeros_like(rope)], axis=-1)
        o_ref[h] = (out * (MLA_QK ** -0.5 * LOG2_E)).astype(o_ref.dtype)


def _q_prep(u, b, l, q_norm, wq, gq, cos_t, sin_t, tm):
    nt = l // tm
    return pl.pallas_call(
        _q_prep_kernel,
        out_shape=jax.ShapeDtypeStruct((b, MLA_HEADS, l, MLA_QK_PAD), BF16),
        grid=(b, nt),
        in_specs=[pl.BlockSpec((tm, MLA_Q_RANK), lambda bi, i: (bi * nt + i, 0)),
                  pl.BlockSpec((1, MLA_Q_RANK), lambda bi, i: (0, 0)),
                  pl.BlockSpec((MLA_Q_RANK, MLA_HEADS * MLA_QK_PAD), lambda bi, i: (0, 0)),
                  pl.BlockSpec((1, MLA_QK_PAD), lambda bi, i: (0, 0)),
                  pl.BlockSpec((tm, MLA_ROPE), lambda bi, i: (i, 0)),
                  pl.BlockSpec((tm, MLA_ROPE), lambda bi, i: (i, 0))],
        out_specs=pl.BlockSpec((None, MLA_HEADS, tm, MLA_QK_PAD), lambda bi, i: (bi, 0, i, 0)),
        compiler_params=_cparams("parallel", "parallel"),
        name="mla_q_prep",
    )(u, q_norm.reshape(1, -1), wq, gq, cos_t, sin_t)


def _kv_prep_kernel(ckv_ref, kvn_ref, wkv_ref, g_ref, cos_ref, sin_ref, k_ref, v_ref):
    cn = _rms(ckv_ref[:, :MLA_KV_RANK].astype(F32), kvn_ref[...]).astype(BF16)
    t_all = jnp.dot(cn, wkv_ref[...], preferred_element_type=F32)
    kr = ckv_ref[:, MLA_KV_RANK:].astype(F32)
    kr_ss = jnp.sum(jnp.where(_lane_iota(kr.shape) < MLA_ROPE, kr * kr, 0.0), axis=-1, keepdims=True)
    g = g_ref[...]
    cos = cos_ref[...]
    sin = sin_ref[...]
    head_w = MLA_NOPE + MLA_V
    for h in range(MLA_HEADS):
        kn = t_all[:, h * head_w:h * head_w + MLA_NOPE]
        ss = jnp.sum(kn * kn, axis=-1, keepdims=True) + kr_ss
        r = lax.rsqrt(ss * (1.0 / MLA_QK) + NORM_EPS)
        kn_n = kn * r * g[:, :MLA_NOPE]
        kr_n = kr * r * g[:, MLA_NOPE:]
        rope = kr_n[:, :MLA_ROPE] * cos + kr_n[:, MLA_ROPE:] * sin
        k_ref[h] = jnp.concatenate([kn_n, rope, jnp.zeros_like(rope)], axis=-1).astype(k_ref.dtype)
        v_ref[h] = t_all[:, h * head_w + MLA_NOPE:(h + 1) * head_w].astype(v_ref.dtype)


def _kv_prep(ckv, kv_norm, wkv, gk, cos_t, sin_t, tm):
    b, m, w = ckv.shape
    return pl.pallas_call(
        _kv_prep_kernel,
        out_shape=(jax.ShapeDtypeStruct((b, MLA_HEADS, m, MLA_QK_PAD), BF16),
                   jax.ShapeDtypeStruct((b, MLA_HEADS, m, MLA_V), BF16)),
        grid=(b, m // tm),
        in_specs=[pl.BlockSpec((None, tm, w), lambda bi, i: (bi, i, 0)),
                  pl.BlockSpec((1, MLA_KV_RANK), lambda bi, i: (0, 0)),
                  pl.BlockSpec(wkv.shape, lambda bi, i: (0, 0)),
                  pl.BlockSpec((1, MLA_QK_PAD), lambda bi, i: (0, 0)),
                  pl.BlockSpec((tm, MLA_ROPE), lambda bi, i: (i, 0)),
                  pl.BlockSpec((tm, MLA_ROPE), lambda bi, i: (i, 0))],
        out_specs=(pl.BlockSpec((None, MLA_HEADS, tm, MLA_QK_PAD), lambda bi, i: (bi, 0, i, 0)),
                   pl.BlockSpec((None, MLA_HEADS, tm, MLA_V), lambda bi, i: (bi, 0, i, 0))),
        compiler_params=_cparams("parallel", "parallel"),
        name="mla_kv_prep",
    )(ckv, kv_norm.reshape(1, -1), wkv, gk, cos_t, sin_t)


def _mla_kernel(*refs, tk, nk, n_cast):
    q_ref, k_ref, v_ref = refs[:3]
    cast_in = refs[3:3 + n_cast]
    o_ref = refs[3 + n_cast]
    cast_out = refs[4 + n_cast:4 + 2 * n_cast]
    sa_scr, sb_scr, m_scr, l_scr, acc_scr = refs[4 + 2 * n_cast:]
    tq = q_ref.shape[0]

    for w_ref, wb_ref in zip(cast_in, cast_out):
        wb_ref[...] = w_ref[...].astype(BF16)

    m_scr[...] = jnp.full_like(m_scr, NEG_BIG)
    l_scr[...] = jnp.zeros_like(l_scr)
    acc_scr[...] = jnp.zeros_like(acc_scr)

    def scores(c, dst):
        off = pl.multiple_of(c * tk, tk)
        dst[...] = _dot_nt(q_ref[...], k_ref[pl.ds(off, tk), :])

    def consume(c, src):
        off = pl.multiple_of(c * tk, tk)
        m_prev = m_scr[...]
        m_new = jnp.maximum(m_prev, jnp.max(src[...], axis=-1, keepdims=True))
        alpha = jnp.exp2(m_prev - m_new)
        psum = jnp.zeros((tq, 1), F32)
        pv = jnp.zeros((tq, MLA_V), F32)
        for j in range(0, tk, MLA_PV_TILE):
            p = jnp.exp2((src[:, j:j + MLA_PV_TILE] - m_new).astype(BF16))
            psum = psum + jnp.sum(p.astype(F32), axis=-1, keepdims=True)
            pv = pv + jnp.dot(p, v_ref[pl.ds(off + j, MLA_PV_TILE), :], preferred_element_type=F32)
        l_scr[...] = alpha * l_scr[...] + psum
        acc_scr[...] = alpha * acc_scr[...] + pv
        m_scr[...] = m_new

    scores(0, sa_scr)

    def body(pair, carry):
        c = 2 * pair
        scores(c + 1, sb_scr)
        consume(c, sa_scr)
        scores(c + 2, sa_scr)
        consume(c + 1, sb_scr)
        return carry

    lax.fori_loop(0, (nk - 1) // 2, body, 0)
    if nk % 2 == 1:
        consume(nk - 1, sa_scr)
    else:
        scores(nk - 1, sb_scr)
        consume(nk - 2, sa_scr)
        consume(nk - 1, sb_scr)
    o_ref[...] = (acc_scr[...] / l_scr[...]).astype(o_ref.dtype)


def _mla_attention(q, k, v, tq, cast_slabs):
    b, h, l, _ = q.shape
    m = k.shape[2]
    nq = l // tq
    steps = b * h * nq
    tk = next(t for t in (1280, 1024, 768, 512, 256, 128) if m % t == 0)
    assert tk % MLA_PV_TILE == 0

    def slab_spec(w):
        rows = w.shape[0] // steps
        assert rows * steps == w.shape[0] and rows % (2 * V7X_SUBLANES) == 0
        return pl.BlockSpec((rows, w.shape[1]), lambda bi, hi, i: ((bi * h + hi) * nq + i, 0))

    slab_specs = [slab_spec(w) for w in cast_slabs]
    resident = dict(pipeline_mode=pl.Buffered(1))
    out = pl.pallas_call(
        functools.partial(_mla_kernel, tk=tk, nk=m // tk, n_cast=len(cast_slabs)),
        out_shape=[jax.ShapeDtypeStruct((b, l, h * MLA_V), BF16)]
        + [jax.ShapeDtypeStruct(w.shape, BF16) for w in cast_slabs],
        grid=(b, h, nq),
        in_specs=[pl.BlockSpec((None, None, tq, MLA_QK_PAD), lambda bi, hi, i: (bi, hi, i, 0)),
                  pl.BlockSpec((None, None, m, MLA_QK_PAD), lambda bi, hi, i: (bi, hi, 0, 0),
                               **resident),
                  pl.BlockSpec((None, None, m, MLA_V), lambda bi, hi, i: (bi, hi, 0, 0),
                               **resident)] + slab_specs,
        out_specs=[pl.BlockSpec((None, tq, MLA_V), lambda bi, hi, i: (bi, i, hi))] + slab_specs,
        scratch_shapes=[pltpu.VMEM((tq, tk), F32), pltpu.VMEM((tq, tk), F32),
                        pltpu.VMEM((tq, 1), F32), pltpu.VMEM((tq, 1), F32),
                        pltpu.VMEM((tq, MLA_V), F32)],
        compiler_params=_cparams("parallel", "parallel", "arbitrary"),
        name="mla_attention",
    )(q, k, v, *cast_slabs)
    return out[0], out[1:]


def _na_kernel(q_ref, k_ref, v_ref, kc_ref, vc_ref, gq_ref, gk_ref, bias_ref, o_ref, *, rows):
    rb = pl.program_id(2)
    k_row0 = jnp.clip(rb * NA_Q_ROWS - NA_WIN_ROWS // 2, 0, rows - NA_K_ROWS)
    off = pl.multiple_of(k_row0 * GRID_W, NA_WIN_ROWS // 2 * GRID_W)
    nk = NA_K_ROWS * GRID_W
    q = (_rms(q_ref[...].astype(F32), gq_ref[...]) * (NA_DIM ** -0.5)).astype(BF16)
    kw = _rms(k_ref[pl.ds(off, nk), :].astype(F32), gk_ref[...]).astype(BF16)
    kc = _rms(kc_ref[...].astype(F32), gk_ref[...]).astype(BF16)
    s_w = _dot_nt(q, kw) + bias_ref[...]
    s_c = _dot_nt(q, kc)
    m = jnp.maximum(jnp.max(s_w, axis=-1, keepdims=True), jnp.max(s_c, axis=-1, keepdims=True))
    p_w = jnp.exp(s_w - m)
    p_c = jnp.exp(s_c - m)
    denom = jnp.sum(p_w, axis=-1, keepdims=True) + jnp.sum(p_c, axis=-1, keepdims=True)
    o = (jnp.dot(p_w.astype(BF16), v_ref[pl.ds(off, nk), :], preferred_element_type=F32)
         + jnp.dot(p_c.astype(BF16), vc_ref[...], preferred_element_type=F32))
    o_ref[...] = (o / denom).astype(o_ref.dtype)


def _na_bias_table(rel_bias, rows):
    nrb = rows // NA_Q_ROWS
    tables = []
    for rb in (0, min(1, nrb - 1), nrb - 1):
        k_row0 = int(np.clip(rb * NA_Q_ROWS - NA_WIN_ROWS // 2, 0, rows - NA_K_ROWS))
        r = rb * NA_Q_ROWS + np.arange(NA_Q_ROWS)
        kr = k_row0 + np.arange(NA_K_ROWS)
        qc = np.arange(GRID_W)
        kc = np.arange(GRID_W)
        rs = np.clip(r - NA_WIN_ROWS // 2, 0, rows - NA_WIN_ROWS)
        cs = np.clip(qc - NA_WIN_COLS // 2, 0, GRID_W - NA_WIN_COLS)
        row_ok = (kr[None, :] >= rs[:, None]) & (kr[None, :] < rs[:, None] + NA_WIN_ROWS)
        col_ok = (kc[None, :] >= cs[:, None]) & (kc[None, :] < cs[:, None] + NA_WIN_COLS)
        ri = np.clip(kr[None, :] - r[:, None] + NA_WIN_ROWS - 1, 0, 2 * NA_WIN_ROWS - 2)
        ci = np.clip(kc[None, :] - qc[:, None] + NA_WIN_COLS - 1, 0, 2 * NA_WIN_COLS - 2)
        sel_r = (ri.reshape(-1)[:, None] == np.arange(2 * NA_WIN_ROWS - 1)[None, :]).astype(np.float32)
        sel_c = (np.arange(2 * NA_WIN_COLS - 1)[:, None] == ci.reshape(-1)[None, :]).astype(np.float32)
        bias = jnp.einsum("rm,hmc->hrc", sel_r, rel_bias.astype(F32), precision=lax.Precision.HIGHEST)
        bias = jnp.einsum("hrc,cq->hrq", bias, sel_c, precision=lax.Precision.HIGHEST)
        bias = bias.reshape(NA_HEADS, NA_Q_ROWS, NA_K_ROWS, GRID_W, GRID_W).transpose(0, 1, 3, 2, 4)
        ok = row_ok[:, None, :, None] & col_ok[None, :, None, :]
        bias = jnp.where(ok[None], bias.astype(F32), NEG_BIG)
        tables.append(bias.reshape(NA_HEADS, NA_Q_ROWS * GRID_W, NA_K_ROWS * GRID_W))
    return jnp.stack(tables)


def _na_attention(u3, uctx3, gq, gk, bias):
    b, l, _ = u3.shape
    n_ctx = uctx3.shape[1]
    rows = l // GRID_W
    nrb = rows // NA_Q_ROWS
    tq = NA_Q_ROWS * GRID_W

    def pattern(rb):
        return jnp.where(rb == 0, 0, jnp.where(rb == nrb - 1, 2, 1))

    return pl.pallas_call(
        functools.partial(_na_kernel, rows=rows),
        out_shape=jax.ShapeDtypeStruct((b, l, NA_HEADS * NA_DIM), BF16),
        grid=(b, NA_HEADS, nrb),
        in_specs=[pl.BlockSpec((None, tq, NA_DIM), lambda bi, h, rb: (bi, rb, U0_NAQ_BLK + h)),
                  pl.BlockSpec((None, l, NA_DIM), lambda bi, h, rb: (bi, 0, U0_NAK_BLK + h)),
                  pl.BlockSpec((None, l, NA_DIM), lambda bi, h, rb: (bi, 0, U0_NAV_BLK + h)),
                  pl.BlockSpec((None, n_ctx, NA_DIM), lambda bi, h, rb: (bi, 0, U0_NAK_BLK + h)),
                  pl.BlockSpec((None, n_ctx, NA_DIM), lambda bi, h, rb: (bi, 0, U0_NAV_BLK + h)),
                  pl.BlockSpec((1, NA_DIM), lambda bi, h, rb: (0, 0)),
                  pl.BlockSpec((1, NA_DIM), lambda bi, h, rb: (0, 0)),
                  pl.BlockSpec((None, None, tq, NA_K_ROWS * GRID_W),
                               lambda bi, h, rb: (pattern(rb), h, 0, 0))],
        out_specs=pl.BlockSpec((None, tq, NA_DIM), lambda bi, h, rb: (bi, rb, h)),
        compiler_params=_cparams("parallel", "parallel", "arbitrary"),
        name="neighbourhood_attention",
    )(u3, u3, u3, uctx3, uctx3, gq.reshape(1, -1), gk.reshape(1, -1), bias)


def _attn_out_kernel(am_ref, an_ref, wm_ref, wn_ref, x_ref, gate_ref, o_ref):
    y = (jnp.dot(am_ref[...], wm_ref[...], preferred_element_type=F32)
         + jnp.dot(an_ref[...], wn_ref[...], preferred_element_type=F32))
    o_ref[...] = x_ref[...] + gate_ref[...] * y


def _attn_out(a_m, a_n, w_out_bf16, x2, mod3, gate_blk, tiles_per_batch, tm):
    n, d = x2.shape
    half = a_m.shape[1]
    return pl.pallas_call(
        _attn_out_kernel,
        out_shape=jax.ShapeDtypeStruct((n, d), F32),
        grid=(n // tm,),
        in_specs=[pl.BlockSpec((tm, half), lambda i: (i, 0)),
                  pl.BlockSpec((tm, half), lambda i: (i, 0)),
                  pl.BlockSpec((half, d), lambda i: (0, 0)),
                  pl.BlockSpec((half, d), lambda i: (1, 0)),
                  pl.BlockSpec((tm, d), lambda i: (i, 0)),
                  pl.BlockSpec((None, 1, d), lambda i: (i // tiles_per_batch, 0, gate_blk))],
        out_specs=pl.BlockSpec((tm, d), lambda i: (i, 0)),
        compiler_params=_cparams("parallel"),
        name="attn_out_residual",
    )(a_m, a_n, w_out_bf16, w_out_bf16, x2, mod3)


def _pool_conv_kernel(prev_ref, u_ref, next_ref, pw_ref, ps_ref, cw_ref, wp_ref, wc_ref, x_ref,
                      gate_ref, o_ref, *, seq_len, tiles_per_batch):
    tm = u_ref.shape[0]
    n_ext = tm + 2 * HALO
    pos0 = (pl.program_id(0) % tiles_per_batch) * tm
    pos_ext = pos0 - HALO + lax.broadcasted_iota(jnp.int32, (n_ext, 1), 0)
    valid = (pos_ext >= 0) & (pos_ext < seq_len)

    def ext(c0, c1):
        e = jnp.concatenate([prev_ref[:, c0:c1], u_ref[:, c0:c1], next_ref[:, c0:c1]], axis=0)
        return jnp.where(valid, e.astype(F32), 0.0)

    def shifted(a, d):
        return pltpu.roll(a, (-d) % n_ext, axis=0)

    pos = (pos0 + lax.broadcasted_iota(jnp.int32, (tm, 1), 0)).astype(F32)
    y = jnp.zeros((tm, x_ref.shape[1]), F32)
    for g, w in enumerate(POOL_WINDOWS):
        c0 = g * POOL_GROUP
        e = ext(c0, c0 + POOL_GROUP)
        acc = e + shifted(e, -1)
        span = 1
        while 2 * span < w:
            acc = shifted(acc, span) + shifted(acc, -span)
            span *= 2
        half = w // 2
        cnt = jnp.minimum(pos + half, float(seq_len)) - jnp.maximum(pos - half, 0.0)
        pooled = acc[HALO:HALO + tm] / cnt - e[HALO:HALO + tm]
        mixed = jnp.dot(pooled.astype(BF16), pw_ref[g], preferred_element_type=F32)
        y_g = (mixed * ps_ref[:, c0:c0 + POOL_GROUP]).astype(BF16)
        y = y + jnp.dot(y_g, wp_ref[c0:c0 + POOL_GROUP, :], preferred_element_type=F32)

    gb0 = POOL_WIDTH
    gc0 = POOL_WIDTH + CONV_CH
    v0 = POOL_WIDTH + 2 * CONV_CH
    z = ext(gc0, gc0 + CONV_CH) * ext(v0, v0 + CONV_CH)
    cw = cw_ref[...]
    conv = cw[0:1] * shifted(z, -1) + cw[1:2] * z + cw[2:3] * shifted(z, 1)
    y_conv = u_ref[:, gb0:gb0 + CONV_CH].astype(F32) * conv[HALO:HALO + tm]
    y = y + jnp.dot(y_conv.astype(BF16), wc_ref[...], preferred_element_type=F32)
    o_ref[...] = x_ref[...] + gate_ref[...] * y


def _pool_conv_mixer(u, pool_w, pool_scale, conv_w, w_out_bf16, x2, mod3, gate_blk, seq_len, tm):
    n, d = x2.shape
    width = u.shape[1]
    tiles_per_batch = seq_len // tm
    hb = tm // HALO
    last = n // HALO - 1
    return pl.pallas_call(
        functools.partial(_pool_conv_kernel, seq_len=seq_len, tiles_per_batch=tiles_per_batch),
        out_shape=jax.ShapeDtypeStruct((n, d), F32),
        grid=(n // tm,),
        in_specs=[pl.BlockSpec((HALO, width), lambda i: (jnp.maximum(i * hb - 1, 0), 0)),
                  pl.BlockSpec((tm, width), lambda i: (i, 0)),
                  pl.BlockSpec((HALO, width), lambda i: (jnp.minimum((i + 1) * hb, last), 0)),
                  pl.BlockSpec(pool_w.shape, lambda i: (0, 0, 0)),
                  pl.BlockSpec((1, POOL_WIDTH), lambda i: (0, 0)),
                  pl.BlockSpec((V7X_SUBLANES, CONV_CH), lambda i: (0, 0)),
                  pl.BlockSpec((POOL_WIDTH, d), lambda i: (0, 0)),
                  pl.BlockSpec((CONV_CH, d), lambda i: (1, 0)),
                  pl.BlockSpec((tm, d), lambda i: (i, 0)),
                  pl.BlockSpec((None, 1, d), lambda i: (i // tiles_per_batch, 0, gate_blk))],
        out_specs=pl.BlockSpec((tm, d), lambda i: (i, 0)),
        compiler_params=_cparams("parallel"),
        name="pool_conv_mixer",
    )(u, u, u, pool_w.astype(BF16), pool_scale.reshape(1, -1),
      jnp.pad(conv_w, ((0, V7X_SUBLANES - conv_w.shape[0]), (0, 0))), w_out_bf16, w_out_bf16, x2, mod3)


def _split_bf16(a):
    hi = a.astype(BF16)
    return hi, (a - hi.astype(F32)).astype(BF16)


def _pack_bf16_pairs(lo, hi):
    lo_bits = lax.bitcast_convert_type(lo.astype(BF16).astype(F32), jnp.uint32) >> 16
    hi_bits = lax.bitcast_convert_type(hi.astype(BF16).astype(F32), jnp.uint32) & jnp.uint32(0xFFFF0000)
    return hi_bits | lo_bits


def _unpack_pairs_f32(w):
    lo = lax.bitcast_convert_type(w << 16, F32)
    hi = lax.bitcast_convert_type(w & jnp.uint32(0xFFFF0000), F32)
    return lo, hi


def _unpack_bf16_pairs(w):
    lo, hi = _unpack_pairs_f32(w)
    return lo.astype(BF16), hi.astype(BF16)


def _router_kernel(x_ref, g_ref, sh_ref, sc_ref, wr_ref, br_ref, tri_ref, hp_ref, idx_ref, gate_ref,
                   rank_ref, cnt_ref, run_scr, *, p_rows):
    tm, d = x_ref.shape

    @pl.when(pl.program_id(0) == 0)
    def _():
        run_scr[...] = jnp.zeros_like(run_scr)

    h = _rms(x_ref[...], g_ref[...]) * (1.0 + sc_ref[...]) + sh_ref[...]
    packed = _pack_bf16_pairs(h[:, :d // 2], h[:, d // 2:])
    for s in range(p_rows):
        hp_ref[pl.ds(s, tm, stride=p_rows), :] = packed[:, s * V7X_LANES:(s + 1) * V7X_LANES]
    h_hi, h_lo = _split_bf16(h)
    w_hi, w_lo = _split_bf16(wr_ref[...])
    logits = _dot_nt(w_hi, h_hi) + _dot_nt(w_hi, h_lo) + _dot_nt(w_lo, h_hi) + br_ref[...]
    expert = lax.broadcasted_iota(jnp.int32, logits.shape, 0)
    vals, ids = [], []
    for _ in range(TOP_K):
        mx = jnp.max(logits, axis=0, keepdims=True)
        am = jnp.min(jnp.where(logits == mx, expert, N_EXPERTS), axis=0, keepdims=True)
        vals.append(mx)
        ids.append(am)
        logits = jnp.where(expert == am, -jnp.inf, logits)
    ex = [jnp.exp(v - vals[0]) for v in vals]
    tot = ex[0] + ex[1] + ex[2] + ex[3]
    idx_ref[...] = jnp.concatenate(ids, axis=0)
    gate_ref[...] = jnp.concatenate([e / tot for e in ex], axis=0)

    running = run_scr[...]
    ranks = []
    for k in range(TOP_K):
        hit = expert == ids[k]
        onehot = jnp.where(hit, 1.0, 0.0)
        before = jnp.dot(onehot.astype(BF16), tri_ref[...], preferred_element_type=F32)
        rank = jnp.sum(jnp.where(hit, before + running, 0.0), axis=0, keepdims=True)
        ranks.append(rank.astype(jnp.int32))
        running = running + jnp.sum(onehot, axis=1, keepdims=True)
    run_scr[...] = running
    rank_ref[...] = jnp.concatenate(ranks, axis=0)
    cnt_ref[...] = jnp.broadcast_to(running, cnt_ref.shape).astype(jnp.int32)


def _router(x2, gain, mod3, shift_blk, scale_blk, tiles_per_batch, w_router, b_router, tm):
    n, d = x2.shape
    p_rows = d // 2 // V7X_LANES
    slot = jax.ShapeDtypeStruct((TOP_K, n), jnp.int32)
    slot_spec = pl.BlockSpec((TOP_K, tm), lambda i: (0, i))
    tri = jnp.triu(jnp.ones((tm, tm), BF16), k=1)
    return pl.pallas_call(
        functools.partial(_router_kernel, p_rows=p_rows),
        out_shape=(jax.ShapeDtypeStruct((n * p_rows, V7X_LANES), jnp.uint32),
                   slot, jax.ShapeDtypeStruct((TOP_K, n), F32), slot,
                   jax.ShapeDtypeStruct((N_EXPERTS, V7X_LANES), jnp.int32)),
        grid=(n // tm,),
        in_specs=[pl.BlockSpec((tm, d), lambda i: (i, 0)),
                  pl.BlockSpec((1, d), lambda i: (0, 0)),
                  pl.BlockSpec((None, 1, d), lambda i: (i // tiles_per_batch, 0, shift_blk)),
                  pl.BlockSpec((None, 1, d), lambda i: (i // tiles_per_batch, 0, scale_blk)),
                  pl.BlockSpec((N_EXPERTS, d), lambda i: (0, 0)),
                  pl.BlockSpec((N_EXPERTS, 1), lambda i: (0, 0)),
                  pl.BlockSpec((tm, tm), lambda i: (0, 0))],
        out_specs=(pl.BlockSpec((tm * p_rows, V7X_LANES), lambda i: (i, 0)),
                   slot_spec, slot_spec, slot_spec,
                   pl.BlockSpec((N_EXPERTS, V7X_LANES), lambda i: (0, 0))),
        scratch_shapes=[pltpu.VMEM((N_EXPERTS, 1), F32)],
        compiler_params=_cparams("arbitrary"),
        name="moe_router",
    )(x2, gain.reshape(1, d), mod3, mod3, w_router.T, b_router.reshape(-1, 1), tri)


def _slot_pos_kernel(pstart_ref, idx_ref, rank_ref, pos_ref):
    idx = idx_ref[...]
    pos = rank_ref[...]
    for e in range(N_EXPERTS):
        pos = pos + jnp.where(idx == e, pstart_ref[e], 0)
    pos_ref[...] = pos


def _slot_positions(pstart, idx_t, rank_t):
    k, n = idx_t.shape
    tn = next(t for t in (4096, 2048, 1024, 512) if n % t == 0)
    spec = pl.BlockSpec((k, tn), lambda i, ps: (0, i))
    return pl.pallas_call(
        _slot_pos_kernel,
        out_shape=jax.ShapeDtypeStruct((k, n), jnp.int32),
        grid_spec=pltpu.PrefetchScalarGridSpec(
            num_scalar_prefetch=1, grid=(n // tn,), in_specs=[spec, spec], out_specs=spec),
        compiler_params=_cparams("parallel"),
        name="moe_slot_positions",
    )(pstart, idx_t, rank_t)


def _row_dma_burst(idx_hbm_row, idx_smem, idx_sem, row_copy):
    n = idx_smem.shape[0]
    cp = pltpu.make_async_copy(idx_hbm_row, idx_smem, idx_sem)
    cp.start()
    cp.wait()

    def issue(pair, c):
        r = 2 * pair
        row_copy(r, idx_smem[r]).start(priority=0)
        row_copy(r + 1, idx_smem[r + 1]).start(priority=1)
        return c

    def drain(r, c):
        row_copy(r, 0).wait()
        return c

    lax.fori_loop(0, n // 2, issue, 0, unroll=4)
    lax.fori_loop(0, n, drain, 0, unroll=8)


def _dispatch_kernel(pos_hbm, hp_ref, xs_zero_hbm, xs_hbm, idx_smem, sem, idx_sem, *, p_rows):
    del xs_zero_hbm
    tt = hp_ref.shape[0] // p_rows

    def row_copy(r, dst_row):
        t = r & (tt - 1)
        src = hp_ref.at[pl.ds(pl.multiple_of(t * p_rows, p_rows), p_rows)]
        return pltpu.make_async_copy(src, xs_hbm.at[dst_row], sem)

    _row_dma_burst(pos_hbm.at[pl.program_id(0)], idx_smem, idx_sem, row_copy)


def _dispatch(pos_tiles, h_packed, n_rows, p_rows):
    n_tiles, slots = pos_tiles.shape
    tt = slots // TOP_K
    assert tt & (tt - 1) == 0
    xs_shape = jax.ShapeDtypeStruct((n_rows, p_rows, V7X_LANES), jnp.uint32)
    return pl.pallas_call(
        functools.partial(_dispatch_kernel, p_rows=p_rows),
        out_shape=xs_shape,
        grid=(n_tiles,),
        in_specs=[pl.BlockSpec(memory_space=pl.ANY),
                  pl.BlockSpec((tt * p_rows, V7X_LANES), lambda i: (i, 0)),
                  pl.BlockSpec(memory_space=pl.ANY)],
        out_specs=pl.BlockSpec(memory_space=pl.ANY),
        scratch_shapes=[pltpu.SMEM((slots,), jnp.int32),
                        pltpu.SemaphoreType.DMA,
                        pltpu.SemaphoreType.DMA],
        input_output_aliases={2: 0},
        compiler_params=_cparams("arbitrary"),
        name="moe_dispatch",
    )(pos_tiles, h_packed, jnp.zeros(xs_shape.shape, xs_shape.dtype))


def _experts_kernel(be_ref, nused_ref, xs_ref, wg_ref, wu_ref, bg_ref, bu_ref, wd_ref, bd_ref,
                    y_ref, x_scr, acc_scr, *, p_rows):
    i = pl.program_id(0)
    j = pl.program_id(1)
    tb, d = x_scr.shape
    active = i < nused_ref[0]

    @pl.when(active & (j == 0))
    def _():
        for s in range(p_rows):
            lo, hi = _unpack_bf16_pairs(xs_ref[pl.ds(s, tb, stride=p_rows), :])
            x_scr[:, s * V7X_LANES:(s + 1) * V7X_LANES] = lo
            x_scr[:, d // 2 + s * V7X_LANES:d // 2 + (s + 1) * V7X_LANES] = hi
        acc_scr[...] = jnp.zeros_like(acc_scr)

    @pl.when(active)
    def _():
        x = x_scr[...]
        a_gate = jnp.dot(x, wg_ref[...], preferred_element_type=F32) + bg_ref[...]
        a_up = jnp.dot(x, wu_ref[...], preferred_element_type=F32) + bu_ref[...]
        a_gate = jnp.minimum(a_gate, SWIGLU_LIMIT)
        a_up = jnp.clip(a_up, -SWIGLU_LIMIT, SWIGLU_LIMIT)
        act = (a_up + 1.0) * (a_gate * jax.nn.sigmoid(SWIGLU_ALPHA * a_gate))
        acc_scr[...] += jnp.dot(act.astype(BF16), wd_ref[...], preferred_element_type=F32)

    last = j == pl.num_programs(1) - 1

    @pl.when(active & last)
    def _():
        y = acc_scr[...] + bd_ref[...]
        packed = _pack_bf16_pairs(y[:, :d // 2], y[:, d // 2:])
        for s in range(p_rows):
            y_ref[pl.ds(s, tb, stride=p_rows), :] = packed[:, s * V7X_LANES:(s + 1) * V7X_LANES]

    @pl.when(jnp.logical_not(active) & last)
    def _():
        y_ref[...] = jnp.zeros_like(y_ref)


def _experts(block_expert, n_used, xs_rows, tb, w_gate_up, b_gate_up, w_down, b_down):
    e, d, two_f = w_gate_up.shape
    p_rows = d // 2 // V7X_LANES
    n_blocks = xs_rows.shape[0] // (tb * p_rows)
    f = two_f // 2
    tf = MOE_F_TILE
    nf = f // tf
    row_spec = pl.BlockSpec((tb * p_rows, V7X_LANES), lambda i, j, be, nu: (i, 0))

    def fcol(i, j, nused):
        return jnp.where(i < nused[0], j, nf - 1)

    grid_spec = pltpu.PrefetchScalarGridSpec(
        num_scalar_prefetch=2,
        grid=(n_blocks, nf),
        in_specs=[row_spec,
                  pl.BlockSpec((None, d, tf), lambda i, j, be, nu: (be[i], 0, fcol(i, j, nu))),
                  pl.BlockSpec((None, d, tf), lambda i, j, be, nu: (be[i], 0, nf + fcol(i, j, nu))),
                  pl.BlockSpec((None, 1, tf), lambda i, j, be, nu: (be[i], 0, fcol(i, j, nu))),
                  pl.BlockSpec((None, 1, tf), lambda i, j, be, nu: (be[i], 0, nf + fcol(i, j, nu))),
                  pl.BlockSpec((None, tf, d), lambda i, j, be, nu: (be[i], fcol(i, j, nu), 0)),
                  pl.BlockSpec((None, 1, d), lambda i, j, be, nu: (be[i], 0, 0))],
        out_specs=row_spec,
        scratch_shapes=[pltpu.VMEM((tb, d), BF16),
                        pltpu.VMEM((tb, d), F32)])
    return pl.pallas_call(
        functools.partial(_experts_kernel, p_rows=p_rows),
        out_shape=jax.ShapeDtypeStruct(xs_rows.shape, jnp.uint32),
        grid_spec=grid_spec,
        compiler_params=_cparams("arbitrary", "arbitrary"),
        name="moe_experts",
    )(block_expert, n_used, xs_rows, w_gate_up, w_gate_up,
      b_gate_up.reshape(e, 1, two_f), b_gate_up.reshape(e, 1, two_f), w_down,
      b_down.reshape(e, 1, d))


def _combine_kernel(pos_hbm, y_hbm, x_ref, mgate_ref, rgate_ref, o_ref, idx_smem, gbuf, sem,
                    idx_sem, *, p_rows):
    tt, d = x_ref.shape

    def row_copy(r, src_row):
        dst = gbuf.at[pl.ds(pl.multiple_of(r * p_rows, p_rows), p_rows)]
        return pltpu.make_async_copy(y_hbm.at[src_row], dst, sem)

    _row_dma_burst(pos_hbm.at[pl.program_id(0)], idx_smem, idx_sem, row_copy)
    rg = rgate_ref[...]
    gates = [jnp.broadcast_to(rg[:, k:k + 1], (tt, V7X_LANES)) for k in range(TOP_K)]
    for s in range(p_rows):
        lo_tot = jnp.zeros((tt, V7X_LANES), F32)
        hi_tot = jnp.zeros((tt, V7X_LANES), F32)
        for k in range(TOP_K):
            lo, hi = _unpack_pairs_f32(gbuf[pl.ds(k * tt * p_rows + s, tt, stride=p_rows), :])
            lo_tot = lo_tot + gates[k] * lo
            hi_tot = hi_tot + gates[k] * hi
        for base, tot in ((0, lo_tot), (d // 2, hi_tot)):
            cols = slice(base + s * V7X_LANES, base + (s + 1) * V7X_LANES)
            o_ref[:, cols] = x_ref[:, cols] + mgate_ref[:, cols] * tot


def _combine(pos_tiles, y_rows, x2, mod3, gate_blk, router_gates, tiles_per_batch, tt):
    n, d = x2.shape
    p_rows = y_rows.shape[1]
    return pl.pallas_call(
        functools.partial(_combine_kernel, p_rows=p_rows),
        out_shape=jax.ShapeDtypeStruct((n, d), F32),
        grid=(n // tt,),
        in_specs=[pl.BlockSpec(memory_space=pl.ANY),
                  pl.BlockSpec(memory_space=pl.ANY),
                  pl.BlockSpec((tt, d), lambda i: (i, 0)),
                  pl.BlockSpec((None, 1, d), lambda i: (i // tiles_per_batch, 0, gate_blk)),
                  pl.BlockSpec((tt, TOP_K), lambda i: (i, 0))],
        out_specs=pl.BlockSpec((tt, d), lambda i: (i, 0)),
        scratch_shapes=[pltpu.SMEM((TOP_K * tt,), jnp.int32),
                        pltpu.VMEM((TOP_K * tt * p_rows, V7X_LANES), jnp.uint32),
                        pltpu.SemaphoreType.DMA,
                        pltpu.SemaphoreType.DMA],
        compiler_params=_cparams("arbitrary"),
        name="moe_combine",
    )(pos_tiles, y_rows, x2, mod3, router_gates)


def _block_tables(counts, n_slots, tb):
    padded = (counts + tb - 1) // tb * tb
    pend = jnp.cumsum(padded)
    pstart = (pend - padded).astype(jnp.int32)
    n_blocks = -(-n_slots // tb) + N_EXPERTS
    block_start = jnp.arange(n_blocks, dtype=jnp.int32) * tb
    block_expert = jnp.minimum(
        jnp.sum(block_start[:, None] >= pend[None, :], axis=1), N_EXPERTS - 1).astype(jnp.int32)
    n_used = (pend[-1] // tb).astype(jnp.int32).reshape(1)
    return pstart, block_expert, n_used, n_blocks


def _moe(x2, norm2, mod3, tiles_per_batch_fn, w_router, b_router, w_gate_up, b_gate_up, w_down,
         b_down):
    n, d = x2.shape
    p_rows = d // 2 // V7X_LANES
    h_packed, idx_t, gate_t, rank_t, counts = _router(
        x2, norm2, mod3, 3, 4, tiles_per_batch_fn(ROW_TILE), w_router, b_router, ROW_TILE)
    pstart, block_expert, n_used, n_blocks = _block_tables(counts[:, 0], TOP_K * n, MOE_ROWS)
    pos = _slot_positions(pstart, idx_t, rank_t)
    tt = COMBINE_TOKENS
    pos_tiles = pos.reshape(TOP_K, n // tt, tt).transpose(1, 0, 2).reshape(n // tt, TOP_K * tt)
    xs = _dispatch(pos_tiles, h_packed, n_blocks * MOE_ROWS, p_rows)
    y_rows = _experts(block_expert, n_used, xs.reshape(-1, V7X_LANES), MOE_ROWS,
                      w_gate_up.astype(BF16), b_gate_up, w_down.astype(BF16), b_down)
    return _combine(pos_tiles, y_rows.reshape(-1, p_rows, V7X_LANES), x2, mod3, 5, gate_t.T,
                    tiles_per_batch_fn(tt), tt)


def _rope_swap_perm():
    half = MLA_ROPE // 2
    quarter = half // 2
    j = np.arange(MLA_ROPE)
    return (j // half) * half + (j % half + quarter) % half


def _rope_tables(n_tokens):
    t = jnp.arange(n_tokens, dtype=jnp.int32)
    row = (t // GRID_W).astype(F32)
    col = (t % GRID_W).astype(F32)
    n_freq = MLA_ROPE // 4
    inv_freq = ROPE_BASE ** (-jnp.arange(n_freq, dtype=F32) / n_freq)
    ar = row[:, None] * inv_freq
    ac = col[:, None] * inv_freq
    cos_t = jnp.concatenate([jnp.cos(ar), jnp.cos(ar), jnp.cos(ac), jnp.cos(ac)], axis=-1)
    sin_t = jnp.concatenate([-jnp.sin(ar), jnp.sin(ar), -jnp.sin(ac), jnp.sin(ac)], axis=-1)
    return cos_t, sin_t


def kernel(x, c, ctx, c_ctx, l0_w_mod, l0_b_mod, l0_norm1, l0_w_in, l0_mla_q_norm, l0_mla_w_q_up, l0_mla_kv_norm, l0_mla_w_kv_up, l0_mla_qk_q, l0_mla_qk_k, l0_na_qk_q, l0_na_qk_k, l0_na_rel_bias, l0_w_out, l0_norm2, l0_w_router, l0_b_router, l0_w_gate_up, l0_b_gate_up, l0_w_down, l0_b_down, l1_w_mod, l1_b_mod, l1_norm1, l1_w_in, l1_pool_w, l1_pool_scale, l1_conv_w, l1_w_out, l1_norm2, l1_w_router, l1_b_router, l1_w_gate_up, l1_b_gate_up, l1_w_down, l1_b_down):
    b, l, d = x.shape
    n_ctx = ctx.shape[1]
    n = b * l
    assert d % V7X_LANES == 0 and l % ROW_TILE == 0 and l % (NA_Q_ROWS * GRID_W) == 0
    assert l // GRID_W >= NA_K_ROWS and b + 1 <= V7X_SUBLANES
    x2 = x.reshape(n, d)

    def tiles_per_batch(tm):
        return l // tm

    c_rows = jnp.concatenate(
        [c, c_ctx[None], jnp.zeros((V7X_SUBLANES - b - 1, d), F32)], axis=0)
    perm = _rope_swap_perm()

    mod3 = _modulation(c_rows, l0_w_mod, l0_b_mod).reshape(V7X_SUBLANES, 1, 6 * d)
    kr0 = MLA_Q_RANK + MLA_KV_RANK
    kr1 = kr0 + MLA_ROPE
    w_in = jnp.concatenate(
        [l0_w_in[:, :kr1], l0_w_in[:, kr0:kr1][:, perm], l0_w_in[:, kr1:],
         jnp.zeros((d, U0_COLS - l0_w_in.shape[1] - MLA_ROPE), F32)], axis=1).astype(BF16)
    tpb = tiles_per_batch(ROW_TILE)
    u = _norm_proj(x2, l0_norm1, mod3, 0, 1, lambda i: i // tpb, w_in, ROW_TILE)
    ctx_tile = n_ctx if (b * n_ctx) % ROW_TILE else ROW_TILE
    u_ctx = _norm_proj(ctx.reshape(b * n_ctx, d), l0_norm1, mod3, 0, 1, lambda i: b, w_in, ctx_tile)

    wq = l0_mla_w_q_up.reshape(MLA_Q_RANK, MLA_HEADS, MLA_QK)
    wq = jnp.concatenate([wq, wq[:, :, MLA_NOPE:][:, :, perm]], axis=-1)
    wq = wq.reshape(MLA_Q_RANK, -1).astype(BF16)
    gq = jnp.concatenate([l0_mla_qk_q, l0_mla_qk_q[MLA_NOPE:][perm]]).reshape(1, -1)
    gk = jnp.concatenate([l0_mla_qk_k, l0_mla_qk_k[MLA_NOPE:][perm]]).reshape(1, -1)
    wkv = l0_mla_w_kv_up.astype(BF16)
    cos_t, sin_t = _rope_tables(l)
    cos_k = jnp.concatenate([cos_t, jnp.ones((n_ctx, MLA_ROPE), F32)], axis=0)
    sin_k = jnp.concatenate([sin_t, jnp.zeros((n_ctx, MLA_ROPE), F32)], axis=0)
    q = _q_prep(u, b, l, l0_mla_q_norm, wq, gq, cos_t, sin_t, ROW_TILE)
    u3 = u.reshape(b, l, U0_COLS)
    uctx3 = u_ctx.reshape(b, n_ctx, U0_COLS)
    ckv0 = U0_CKV_BLK * V7X_LANES
    ckv1 = ckv0 + MLA_KV_RANK + 2 * MLA_ROPE
    ckv = jnp.concatenate([u3[:, :, ckv0:ckv1], uctx3[:, :, ckv0:ckv1]], axis=1)
    k_m, v_m = _kv_prep(ckv, l0_mla_kv_norm, wkv, gk, cos_k, sin_k, n_ctx)
    expert_w = (l0_w_gate_up, l0_w_down, l1_w_gate_up, l1_w_down)
    o_m, expert_w_bf16 = _mla_attention(
        q, k_m, v_m, ROW_TILE, [w.reshape(-1, w.shape[-1]) for w in expert_w])
    l0_w_gate_up, l0_w_down, l1_w_gate_up, l1_w_down = (
        wb.reshape(w.shape) for wb, w in zip(expert_w_bf16, expert_w))

    bias = _na_bias_table(l0_na_rel_bias, l // GRID_W)
    o_n = _na_attention(u3, uctx3, l0_na_qk_q, l0_na_qk_k, bias)

    x2 = _attn_out(o_m.reshape(n, -1), o_n.reshape(n, -1), l0_w_out.astype(BF16), x2, mod3, 2,
                   tpb, ROW_TILE)
    x2 = _moe(x2, l0_norm2, mod3, tiles_per_batch, l0_w_router, l0_b_router, l0_w_gate_up,
              l0_b_gate_up, l0_w_down, l0_b_down)

    mod3 = _modulation(c_rows, l1_w_mod, l1_b_mod).reshape(V7X_SUBLANES, 1, 6 * d)
    u = _norm_proj(x2, l1_norm1, mod3, 0, 1, lambda i: i // tpb, l1_w_in.astype(BF16), ROW_TILE)
    x2 = _pool_conv_mixer(u, l1_pool_w, l1_pool_scale, l1_conv_w, l1_w_out.astype(BF16), x2, mod3,
                          2, l, ROW_TILE)
    x2 = _moe(x2, l1_norm2, mod3, tiles_per_batch, l1_w_router, l1_b_router, l1_w_gate_up,
              l1_b_gate_up, l1_w_down, l1_b_down)
    return x2.reshape(b, l, d)
```

```python
import functools

import numpy as np
import jax
import jax.numpy as jnp
from jax import lax
from jax.experimental import pallas as pl
from jax.experimental.pallas import tpu as pltpu

F32 = jnp.float32
BF16 = jnp.bfloat16

V7X_LANES = 128
V7X_SUBLANES = 8
V7X_VMEM_BYTES = 64 * 2**20
VMEM_LIMIT_BYTES = V7X_VMEM_BYTES * 7 // 8

GRID_W = 64
NORM_EPS = 1e-6
ROPE_BASE = 10000.0
MLA_HEADS = 8
MLA_Q_RANK = 512
MLA_KV_RANK = 256
MLA_NOPE = 128
MLA_ROPE = 64
MLA_V = 128
MLA_QK = MLA_NOPE + MLA_ROPE
MLA_QK_PAD = 256
MLA_PV_TILE = 256
NA_HEADS = 8
NA_DIM = 128
NA_WIN_ROWS = 8
NA_WIN_COLS = 16
NA_Q_ROWS = 8
NA_K_ROWS = 16
POOL_WINDOWS = (2, 4, 8, 16)
POOL_WIDTH = 1024
POOL_GROUP = POOL_WIDTH // len(POOL_WINDOWS)
CONV_CH = 1024
HALO = 16
N_EXPERTS = 32
TOP_K = 4
SWIGLU_LIMIT = 7.0
SWIGLU_ALPHA = 1.702
NEG_BIG = -1e30
LOG2_E = 1.4426950408889634

U0_COLS = 4096
U0_CKV_BLK = 4
U0_NAQ_BLK = 7
U0_NAK_BLK = 15
U0_NAV_BLK = 23

ROW_TILE = 512
MOE_ROWS = 512
MOE_F_TILE = 1024
COMBINE_TOKENS = 256


def _cparams(*sem):
    return pltpu.CompilerParams(dimension_semantics=sem, vmem_limit_bytes=VMEM_LIMIT_BYTES)


def _rms(x, gain):
    ms = jnp.mean(x * x, axis=-1, keepdims=True)
    return x * lax.rsqrt(ms + NORM_EPS) * gain


def _dot_nt(a, b):
    return lax.dot_general(a, b, (((1,), (1,)), ((), ())), preferred_element_type=F32)


def _mod_kernel(c_ref, w_ref, b_ref, o_ref):
    c = c_ref[...]
    s = c * jax.nn.sigmoid(c)
    o_ref[...] = jnp.dot(s, w_ref[...], preferred_element_type=F32,
                         precision=lax.Precision.HIGHEST) + b_ref[...]


def _modulation(c_rows, w_mod, b_mod):
    d, n = w_mod.shape
    tn = 512
    assert n % tn == 0
    return pl.pallas_call(
        _mod_kernel,
        out_shape=jax.ShapeDtypeStruct((V7X_SUBLANES, n), F32),
        grid=(n // tn,),
        in_specs=[pl.BlockSpec((V7X_SUBLANES, d), lambda j: (0, 0)),
                  pl.BlockSpec((d, tn), lambda j: (0, j)),
                  pl.BlockSpec((1, tn), lambda j: (0, j))],
        out_specs=pl.BlockSpec((V7X_SUBLANES, tn), lambda j: (0, j)),
        compiler_params=_cparams("parallel"),
        name="modulation",
    )(c_rows, w_mod, b_mod.reshape(1, n))


def _norm_proj_kernel(x_ref, g_ref, sh_ref, sc_ref, w_ref, o_ref, *, tn):
    h = (_rms(x_ref[...], g_ref[...]) * (1.0 + sc_ref[...]) + sh_ref[...]).astype(BF16)
    for c in range(0, w_ref.shape[1], tn):
        o_ref[:, c:c + tn] = jnp.dot(
            h, w_ref[:, c:c + tn], preferred_element_type=F32).astype(o_ref.dtype)


def _norm_proj(x2, gain, mod3, shift_blk, scale_blk, group_of_tile, w_bf16, tm):
    n, d = x2.shape
    ncol = w_bf16.shape[1]
    return pl.pallas_call(
        functools.partial(_norm_proj_kernel, tn=1024),
        out_shape=jax.ShapeDtypeStruct((n, ncol), BF16),
        grid=(n // tm,),
        in_specs=[pl.BlockSpec((tm, d), lambda i: (i, 0)),
                  pl.BlockSpec((1, d), lambda i: (0, 0)),
                  pl.BlockSpec((None, 1, d), lambda i: (group_of_tile(i), 0, shift_blk)),
                  pl.BlockSpec((None, 1, d), lambda i: (group_of_tile(i), 0, scale_blk)),
                  pl.BlockSpec((d, ncol), lambda i: (0, 0), pipeline_mode=pl.Buffered(1))],
        out_specs=pl.BlockSpec((tm, ncol), lambda i: (i, 0)),
        compiler_params=_cparams("parallel"),
        name="norm_proj",
    )(x2, gain.reshape(1, d), mod3, mod3, w_bf16)


def _lane_iota(shape):
    return lax.broadcasted_iota(jnp.int32, shape, len(shape) - 1)


def _q_prep_kernel(cq_ref, qn_ref, wq_ref, g_ref, cos_ref, sin_ref, o_ref):
    cn = _rms(cq_ref[...].astype(F32), qn_ref[...]).astype(BF16)
    t_all = jnp.dot(cn, wq_ref[...], preferred_element_type=F32)
    in_head = _lane_iota((t_all.shape[0], MLA_QK_PAD)) < MLA_QK
    g = g_ref[...]
    cos = cos_ref[...]
    sin = sin_ref[...]
    for h in range(MLA_HEADS):
        t = t_all[:, h * MLA_QK_PAD:(h + 1) * MLA_QK_PAD]
        sq = jnp.where(in_head, t * t, 0.0)
        r = lax.rsqrt(jnp.sum(sq, axis=-1, keepdims=True) * (1.0 / MLA_QK) + NORM_EPS)
        tn = t * r * g
        rope = tn[:, MLA_NOPE:MLA_QK] * cos + tn[:, MLA_QK:] * sin
        out = jnp.concatenate([tn[:, :MLA_NOPE], rope, jnp.zeros_like(rope)], axis=-1)
        o_ref[h] = (out * (MLA_QK ** -0.5 * LOG2_E)).astype(o_ref.dtype)


def _q_prep(u, b, l, q_norm, wq, gq, cos_t, sin_t, tm):
    nt = l // tm
    return pl.pallas_call(
        _q_prep_kernel,
        out_shape=jax.ShapeDtypeStruct((b, MLA_HEADS, l, MLA_QK_PAD), BF16),
        grid=(b, nt),
        in_specs=[pl.BlockSpec((tm, MLA_Q_RANK), lambda bi, i: (bi * nt + i, 0)),
                  pl.BlockSpec((1, MLA_Q_RANK), lambda bi, i: (0, 0)),
                  pl.BlockSpec((MLA_Q_RANK, MLA_HEADS * MLA_QK_PAD), lambda bi, i: (0, 0)),
                  pl.BlockSpec((1, MLA_QK_PAD), lambda bi, i: (0, 0)),
                  pl.BlockSpec((tm, MLA_ROPE), lambda bi, i: (i, 0)),
                  pl.BlockSpec((tm, MLA_ROPE), lambda bi, i: (i, 0))],
        out_specs=pl.BlockSpec((None, MLA_HEADS, tm, MLA_QK_PAD), lambda bi, i: (bi, 0, i, 0)),
        compiler_params=_cparams("parallel", "parallel"),
        name="mla_q_prep",
    )(u, q_norm.reshape(1, -1), wq, gq, cos_t, sin_t)


def _kv_prep_kernel(ckv_ref, kvn_ref, wkv_ref, g_ref, cos_ref, sin_ref, k_ref, v_ref):
    cn = _rms(ckv_ref[:, :MLA_KV_RANK].astype(F32), kvn_ref[...]).astype(BF16)
    t_all = jnp.dot(cn, wkv_ref[...], preferred_element_type=F32)
    kr = ckv_ref[:, MLA_KV_RANK:].astype(F32)
    kr_ss = jnp.sum(jnp.where(_lane_iota(kr.shape) < MLA_ROPE, kr * kr, 0.0), axis=-1, keepdims=True)
    g = g_ref[...]
    cos = cos_ref[...]
    sin = sin_ref[...]
    head_w = MLA_NOPE + MLA_V
    for h in range(MLA_HEADS):
        kn = t_all[:, h * head_w:h * head_w + MLA_NOPE]
        ss = jnp.sum(kn * kn, axis=-1, keepdims=True) + kr_ss
        r = lax.rsqrt(ss * (1.0 / MLA_QK) + NORM_EPS)
        kn_n = kn * r * g[:, :MLA_NOPE]
        kr_n = kr * r * g[:, MLA_NOPE:]
        rope = kr_n[:, :MLA_ROPE] * cos + kr_n[:, MLA_ROPE:] * sin
        k_ref[h] = jnp.concatenate([kn_n, rope, jnp.zeros_like(rope)], axis=-1).astype(k_ref.dtype)
        v_ref[h] = t_all[:, h * head_w + MLA_NOPE:(h + 1) * head_w].astype(v_ref.dtype)


def _kv_prep(ckv, kv_norm, wkv, gk, cos_t, sin_t, tm):
    b, m, w = ckv.shape
    return pl.pallas_call(
        _kv_prep_kernel,
        out_shape=(jax.ShapeDtypeStruct((b, MLA_HEADS, m, MLA_QK_PAD), BF16),
                   jax.ShapeDtypeStruct((b, MLA_HEADS, m, MLA_V), BF16)),
        grid=(b, m // tm),
        in_specs=[pl.BlockSpec((None, tm, w), lambda bi, i: (bi, i, 0)),
                  pl.BlockSpec((1, MLA_KV_RANK), lambda bi, i: (0, 0)),
                  pl.BlockSpec(wkv.shape, lambda bi, i: (0, 0)),
                  pl.BlockSpec((1, MLA_QK_PAD), lambda bi, i: (0, 0)),
                  pl.BlockSpec((tm, MLA_ROPE), lambda bi, i: (i, 0)),
                  pl.BlockSpec((tm, MLA_ROPE), lambda bi, i: (i, 0))],
        out_specs=(pl.BlockSpec((None, MLA_HEADS, tm, MLA_QK_PAD), lambda bi, i: (bi, 0, i, 0)),
                   pl.BlockSpec((None, MLA_HEADS, tm, MLA_V), lambda bi, i: (bi, 0, i, 0))),
        compiler_params=_cparams("parallel", "parallel"),
        name="mla_kv_prep",
    )(ckv, kv_norm.reshape(1, -1), wkv, gk, cos_t, sin_t)


def _mla_kernel(*refs, tk, nk, n_cast):
    q_ref, k_ref, v_ref = refs[:3]
    cast_in = refs[3:3 + n_cast]
    o_ref = refs[3 + n_cast]
    cast_out = refs[4 + n_cast:4 + 2 * n_cast]
    sa_scr, sb_scr, m_scr, l_scr, acc_scr = refs[4 + 2 * n_cast:]
    tq = q_ref.shape[0]

    for w_ref, wb_ref in zip(cast_in, cast_out):
        wb_ref[...] = w_ref[...].astype(BF16)

    m_scr[...] = jnp.full_like(m_scr, NEG_BIG)
    l_scr[...] = jnp.zeros_like(l_scr)
    acc_scr[...] = jnp.zeros_like(acc_scr)

    def scores(c, dst):
        off = pl.multiple_of(c * tk, tk)
        dst[...] = _dot_nt(q_ref[...], k_ref[pl.ds(off, tk), :])

    def consume(c, src):
        off = pl.multiple_of(c * tk, tk)
        m_prev = m_scr[...]
        m_new = jnp.maximum(m_prev, jnp.max(src[...], axis=-1, keepdims=True))
        alpha = jnp.exp2(m_prev - m_new)
        psum = jnp.zeros((tq, 1), F32)
        pv = jnp.zeros((tq, MLA_V), F32)
        for j in range(0, tk, MLA_PV_TILE):
            p = jnp.exp2((src[:, j:j + MLA_PV_TILE] - m_new).astype(BF16))
            psum = psum + jnp.sum(p.astype(F32), axis=-1, keepdims=True)
            pv = pv + jnp.dot(p, v_ref[pl.ds(off + j, MLA_PV_TILE), :], preferred_element_type=F32)
        l_scr[...] = alpha * l_scr[...] + psum
        acc_scr[...] = alpha * acc_scr[...] + pv
        m_scr[...] = m_new

    scores(0, sa_scr)

    def body(pair, carry):
        c = 2 * pair
        scores(c + 1, sb_scr)
        consume(c, sa_scr)
        scores(c + 2, sa_scr)
        consume(c + 1, sb_scr)
        return carry

    lax.fori_loop(0, (nk - 1) // 2, body, 0)
    if nk % 2 == 1:
        consume(nk - 1, sa_scr)
    else:
        scores(nk - 1, sb_scr)
        consume(nk - 2, sa_scr)
        consume(nk - 1, sb_scr)
    o_ref[...] = (acc_scr[...] / l_scr[...]).astype(o_ref.dtype)


def _mla_attention(q, k, v, tq, cast_slabs):
    b, h, l, _ = q.shape
    m = k.shape[2]
    nq = l // tq
    steps = b * h * nq
    tk = next(t for t in (1280, 1024, 768, 512, 256, 128) if m % t == 0)
    assert tk % MLA_PV_TILE == 0

    def slab_spec(w):
        rows = w.shape[0] // steps
        assert rows * steps == w.shape[0] and rows % (2 * V7X_SUBLANES) == 0
        return pl.BlockSpec((rows, w.shape[1]), lambda bi, hi, i: ((bi * h + hi) * nq + i, 0))

    slab_specs = [slab_spec(w) for w in cast_slabs]
    resident = dict(pipeline_mode=pl.Buffered(1))
    out = pl.pallas_call(
        functools.partial(_mla_kernel, tk=tk, nk=m // tk, n_cast=len(cast_slabs)),
        out_shape=[jax.ShapeDtypeStruct((b, l, h * MLA_V), BF16)]
        + [jax.ShapeDtypeStruct(w.shape, BF16) for w in cast_slabs],
        grid=(b, h, nq),
        in_specs=[pl.BlockSpec((None, None, tq, MLA_QK_PAD), lambda bi, hi, i: (bi, hi, i, 0)),
                  pl.BlockSpec((None, None, m, MLA_QK_PAD), lambda bi, hi, i: (bi, hi, 0, 0),
                               **resident),
                  pl.BlockSpec((None, None, m, MLA_V), lambda bi, hi, i: (bi, hi, 0, 0),
                               **resident)] + slab_specs,
        out_specs=[pl.BlockSpec((None, tq, MLA_V), lambda bi, hi, i: (bi, i, hi))] + slab_specs,
        scratch_shapes=[pltpu.VMEM((tq, tk), F32), pltpu.VMEM((tq, tk), F32),
                        pltpu.VMEM((tq, 1), F32), pltpu.VMEM((tq, 1), F32),
                        pltpu.VMEM((tq, MLA_V), F32)],
        compiler_params=_cparams("parallel", "parallel", "arbitrary"),
        name="mla_attention",
    )(q, k, v, *cast_slabs)
    return out[0], out[1:]


def _na_kernel(q_ref, k_ref, v_ref, kc_ref, vc_ref, gq_ref, gk_ref, bias_ref, o_ref, *, rows):
    rb = pl.program_id(2)
    k_row0 = jnp.clip(rb * NA_Q_ROWS - NA_WIN_ROWS // 2, 0, rows - NA_K_ROWS)
    off = pl.multiple_of(k_row0 * GRID_W, NA_WIN_ROWS // 2 * GRID_W)
    nk = NA_K_ROWS * GRID_W
    q = (_rms(q_ref[...].astype(F32), gq_ref[...]) * (NA_DIM ** -0.5)).astype(BF16)
    kw = _rms(k_ref[pl.ds(off, nk), :].astype(F32), gk_ref[...]).astype(BF16)
    kc = _rms(kc_ref[...].astype(F32), gk_ref[...]).astype(BF16)
    s_w = _dot_nt(q, kw) + bias_ref[...]
    s_c = _dot_nt(q, kc)
    m = jnp.maximum(jnp.max(s_w, axis=-1, keepdims=True), jnp.max(s_c, axis=-1, keepdims=True))
    p_w = jnp.exp(s_w - m)
    p_c = jnp.exp(s_c - m)
    denom = jnp.sum(p_w, axis=-1, keepdims=True) + jnp.sum(p_c, axis=-1, keepdims=True)
    o = (jnp.dot(p_w.astype(BF16), v_ref[pl.ds(off, nk), :], preferred_element_type=F32)
         + jnp.dot(p_c.astype(BF16), vc_ref[...], preferred_element_type=F32))
    o_ref[...] = (o / denom).astype(o_ref.dtype)


def _na_bias_table(rel_bias, rows):
    nrb = rows // NA_Q_ROWS
    tables = []
    for rb in (0, min(1, nrb - 1), nrb - 1):
        k_row0 = int(np.clip(rb * NA_Q_ROWS - NA_WIN_ROWS // 2, 0, rows - NA_K_ROWS))
        r = rb * NA_Q_ROWS + np.arange(NA_Q_ROWS)
        kr = k_row0 + np.arange(NA_K_ROWS)
        qc = np.arange(GRID_W)
        kc = np.arange(GRID_W)
        rs = np.clip(r - NA_WIN_ROWS // 2, 0, rows - NA_WIN_ROWS)
        cs = np.clip(qc - NA_WIN_COLS // 2, 0, GRID_W - NA_WIN_COLS)
        row_ok = (kr[None, :] >= rs[:, None]) & (kr[None, :] < rs[:, None] + NA_WIN_ROWS)
        col_ok = (kc[None, :] >= cs[:, None]) & (kc[None, :] < cs[:, None] + NA_WIN_COLS)
        ri = np.clip(kr[None, :] - r[:, None] + NA_WIN_ROWS - 1, 0, 2 * NA_WIN_ROWS - 2)
        ci = np.clip(kc[None, :] - qc[:, None] + NA_WIN_COLS - 1, 0, 2 * NA_WIN_COLS - 2)
        sel_r = (ri.reshape(-1)[:, None] == np.arange(2 * NA_WIN_ROWS - 1)[None, :]).astype(np.float32)
        sel_c = (np.arange(2 * NA_WIN_COLS - 1)[:, None] == ci.reshape(-1)[None, :]).astype(np.float32)
        bias = jnp.einsum("rm,hmc->hrc", sel_r, rel_bias.astype(F32), precision=lax.Precision.HIGHEST)
        bias = jnp.einsum("hrc,cq->hrq", bias, sel_c, precision=lax.Precision.HIGHEST)
        bias = bias.reshape(NA_HEADS, NA_Q_ROWS, NA_K_ROWS, GRID_W, GRID_W).transpose(0, 1, 3, 2, 4)
        ok = row_ok[:, None, :, None] & col_ok[None, :, None, :]
        bias = jnp.where(ok[None], bias.astype(F32), NEG_BIG)
        tables.append(bias.reshape(NA_HEADS, NA_Q_ROWS * GRID_W, NA_K_ROWS * GRID_W))
    return jnp.stack(tables)


def _na_attention(u3, uctx3, gq, gk, bias):
    b, l, _ = u3.shape
    n_ctx = uctx3.shape[1]
    rows = l // GRID_W
    nrb = rows // NA_Q_ROWS
    tq = NA_Q_ROWS * GRID_W

    def pattern(rb):
        return jnp.where(rb == 0, 0, jnp.where(rb == nrb - 1, 2, 1))

    return pl.pallas_call(
        functools.partial(_na_kernel, rows=rows),
        out_shape=jax.ShapeDtypeStruct((b, l, NA_HEADS * NA_DIM), BF16),
        grid=(b, NA_HEADS, nrb),
        in_specs=[pl.BlockSpec((None, tq, NA_DIM), lambda bi, h, rb: (bi, rb, U0_NAQ_BLK + h)),
                  pl.BlockSpec((None, l, NA_DIM), lambda bi, h, rb: (bi, 0, U0_NAK_BLK + h)),
                  pl.BlockSpec((None, l, NA_DIM), lambda bi, h, rb: (bi, 0, U0_NAV_BLK + h)),
                  pl.BlockSpec((None, n_ctx, NA_DIM), lambda bi, h, rb: (bi, 0, U0_NAK_BLK + h)),
                  pl.BlockSpec((None, n_ctx, NA_DIM), lambda bi, h, rb: (bi, 0, U0_NAV_BLK + h)),
                  pl.BlockSpec((1, NA_DIM), lambda bi, h, rb: (0, 0)),
                  pl.BlockSpec((1, NA_DIM), lambda bi, h, rb: (0, 0)),
                  pl.BlockSpec((None, None, tq, NA_K_ROWS * GRID_W),
                               lambda bi, h, rb: (pattern(rb), h, 0, 0))],
        out_specs=pl.BlockSpec((None, tq, NA_DIM), lambda bi, h, rb: (bi, rb, h)),
        compiler_params=_cparams("parallel", "parallel", "arbitrary"),
        name="neighbourhood_attention",
    )(u3, u3, u3, uctx3, uctx3, gq.reshape(1, -1), gk.reshape(1, -1), bias)


def _attn_out_kernel(am_ref, an_ref, wm_ref, wn_ref, x_ref, gate_ref, o_ref):
    y = (jnp.dot(am_ref[...], wm_ref[...], preferred_element_type=F32)
         + jnp.dot(an_ref[...], wn_ref[...], preferred_element_type=F32))
    o_ref[...] = x_ref[...] + gate_ref[...] * y


def _attn_out(a_m, a_n, w_out_bf16, x2, mod3, gate_blk, tiles_per_batch, tm):
    n, d = x2.shape
    half = a_m.shape[1]
    return pl.pallas_call(
        _attn_out_kernel,
        out_shape=jax.ShapeDtypeStruct((n, d), F32),
        grid=(n // tm,),
        in_specs=[pl.BlockSpec((tm, half), lambda i: (i, 0)),
                  pl.BlockSpec((tm, half), lambda i: (i, 0)),
                  pl.BlockSpec((half, d), lambda i: (0, 0)),
                  pl.BlockSpec((half, d), lambda i: (1, 0)),
                  pl.BlockSpec((tm, d), lambda i: (i, 0)),
                  pl.BlockSpec((None, 1, d), lambda i: (i // tiles_per_batch, 0, gate_blk))],
        out_specs=pl.BlockSpec((tm, d), lambda i: (i, 0)),
        compiler_params=_cparams("parallel"),
        name="attn_out_residual",
    )(a_m, a_n, w_out_bf16, w_out_bf16, x2, mod3)


def _pool_conv_kernel(prev_ref, u_ref, next_ref, pw_ref, ps_ref, cw_ref, wp_ref, wc_ref, x_ref,
                      gate_ref, o_ref, *, seq_len, tiles_per_batch):
    tm = u_ref.shape[0]
    n_ext = tm + 2 * HALO
    pos0 = (pl.program_id(0) % tiles_per_batch) * tm
    pos_ext = pos0 - HALO + lax.broadcasted_iota(jnp.int32, (n_ext, 1), 0)
    valid = (pos_ext >= 0) & (pos_ext < seq_len)

    def ext(c0, c1):
        e = jnp.concatenate([prev_ref[:, c0:c1], u_ref[:, c0:c1], next_ref[:, c0:c1]], axis=0)
        return jnp.where(valid, e.astype(F32), 0.0)

    def shifted(a, d):
        return pltpu.roll(a, (-d) % n_ext, axis=0)

    pos = (pos0 + lax.broadcasted_iota(jnp.int32, (tm, 1), 0)).astype(F32)
    y = jnp.zeros((tm, x_ref.shape[1]), F32)
    for g, w in enumerate(POOL_WINDOWS):
        c0 = g * POOL_GROUP
        e = ext(c0, c0 + POOL_GROUP)
        acc = e + shifted(e, -1)
        span = 1
        while 2 * span < w:
            acc = shifted(acc, span) + shifted(acc, -span)
            span *= 2
        half = w // 2
        cnt = jnp.minimum(pos + half, float(seq_len)) - jnp.maximum(pos - half, 0.0)
        pooled = acc[HALO:HALO + tm] / cnt - e[HALO:HALO + tm]
        mixed = jnp.dot(pooled.astype(BF16), pw_ref[g], preferred_element_type=F32)
        y_g = (mixed * ps_ref[:, c0:c0 + POOL_GROUP]).astype(BF16)
        y = y + jnp.dot(y_g, wp_ref[c0:c0 + POOL_GROUP, :], preferred_element_type=F32)

    gb0 = POOL_WIDTH
    gc0 = POOL_WIDTH + CONV_CH
    v0 = POOL_WIDTH + 2 * CONV_CH
    z = ext(gc0, gc0 + CONV_CH) * ext(v0, v0 + CONV_CH)
    cw = cw_ref[...]
    conv = cw[0:1] * shifted(z, -1) + cw[1:2] * z + cw[2:3] * shifted(z, 1)
    y_conv = u_ref[:, gb0:gb0 + CONV_CH].astype(F32) * conv[HALO:HALO + tm]
    y = y + jnp.dot(y_conv.astype(BF16), wc_ref[...], preferred_element_type=F32)
    o_ref[...] = x_ref[...] + gate_ref[...] * y


def _pool_conv_mixer(u, pool_w, pool_scale, conv_w, w_out_bf16, x2, mod3, gate_blk, seq_len, tm):
    n, d = x2.shape
    width = u.shape[1]
    tiles_per_batch = seq_len // tm
    hb = tm // HALO
    last = n // HALO - 1
    return pl.pallas_call(
        functools.partial(_pool_conv_kernel, seq_len=seq_len, tiles_per_batch=tiles_per_batch),
        out_shape=jax.ShapeDtypeStruct((n, d), F32),
        grid=(n // tm,),
        in_specs=[pl.BlockSpec((HALO, width), lambda i: (jnp.maximum(i * hb - 1, 0), 0)),
                  pl.BlockSpec((tm, width), lambda i: (i, 0)),
                  pl.BlockSpec((HALO, width), lambda i: (jnp.minimum((i + 1) * hb, last), 0)),
                  pl.BlockSpec(pool_w.shape, lambda i: (0, 0, 0)),
                  pl.BlockSpec((1, POOL_WIDTH), lambda i: (0, 0)),
                  pl.BlockSpec((V7X_SUBLANES, CONV_CH), lambda i: (0, 0)),
                  pl.BlockSpec((POOL_WIDTH, d), lambda i: (0, 0)),
                  pl.BlockSpec((CONV_CH, d), lambda i: (1, 0)),
                  pl.BlockSpec((tm, d), lambda i: (i, 0)),
                  pl.BlockSpec((None, 1, d), lambda i: (i // tiles_per_batch, 0, gate_blk))],
        out_specs=pl.BlockSpec((tm, d), lambda i: (i, 0)),
        compiler_params=_cparams("parallel"),
        name="pool_conv_mixer",
    )(u, u, u, pool_w.astype(BF16), pool_scale.reshape(1, -1),
      jnp.pad(conv_w, ((0, V7X_SUBLANES - conv_w.shape[0]), (0, 0))), w_out_bf16, w_out_bf16, x2, mod3)


def _split_bf16(a):
    hi = a.astype(BF16)
    return hi, (a - hi.astype(F32)).astype(BF16)


def _pack_bf16_pairs(lo, hi):
    lo_bits = lax.bitcast_convert_type(lo.astype(BF16).astype(F32), jnp.uint32) >> 16
    hi_bits = lax.bitcast_convert_type(hi.astype(BF16).astype(F32), jnp.uint32) & jnp.uint32(0xFFFF0000)
    return hi_bits | lo_bits


def _unpack_pairs_f32(w):
    lo = lax.bitcast_convert_type(w << 16, F32)
    hi = lax.bitcast_convert_type(w & jnp.uint32(0xFFFF0000), F32)
    return lo, hi


def _unpack_bf16_pairs(w):
    lo, hi = _unpack_pairs_f32(w)
    return lo.astype(BF16), hi.astype(BF16)


def _router_kernel(x_ref, g_ref, sh_ref, sc_ref, wr_ref, br_ref, tri_ref, hp_ref, idx_ref, gate_ref,
                   rank_ref, cnt_ref, run_scr, *, p_rows):
    tm, d = x_ref.shape

    @pl.when(pl.program_id(0) == 0)
    def _():
        run_scr[...] = jnp.zeros_like(run_scr)

    h = _rms(x_ref[...], g_ref[...]) * (1.0 + sc_ref[...]) + sh_ref[...]
    packed = _pack_bf16_pairs(h[:, :d // 2], h[:, d // 2:])
    for s in range(p_rows):
        hp_ref[pl.ds(s, tm, stride=p_rows), :] = packed[:, s * V7X_LANES:(s + 1) * V7X_LANES]
    h_hi, h_lo = _split_bf16(h)
    w_hi, w_lo = _split_bf16(wr_ref[...])
    logits = _dot_nt(w_hi, h_hi) + _dot_nt(w_hi, h_lo) + _dot_nt(w_lo, h_hi) + br_ref[...]
    expert = lax.broadcasted_iota(jnp.int32, logits.shape, 0)
    vals, ids = [], []
    for _ in range(TOP_K):
        mx = jnp.max(logits, axis=0, keepdims=True)
        am = jnp.min(jnp.where(logits == mx, expert, N_EXPERTS), axis=0, keepdims=True)
        vals.append(mx)
        ids.append(am)
        logits = jnp.where(expert == am, -jnp.inf, logits)
    ex = [jnp.exp(v - vals[0]) for v in vals]
    tot = ex[0] + ex[1] + ex[2] + ex[3]
    idx_ref[...] = jnp.concatenate(ids, axis=0)
    gate_ref[...] = jnp.concatenate([e / tot for e in ex], axis=0)

    running = run_scr[...]
    ranks = []
    for k in range(TOP_K):
        hit = expert == ids[k]
        onehot = jnp.where(hit, 1.0, 0.0)
        before = jnp.dot(onehot.astype(BF16), tri_ref[...], preferred_element_type=F32)
        rank = jnp.sum(jnp.where(hit, before + running, 0.0), axis=0, keepdims=True)
        ranks.append(rank.astype(jnp.int32))
        running = running + jnp.sum(onehot, axis=1, keepdims=True)
    run_scr[...] = running
    rank_ref[...] = jnp.concatenate(ranks, axis=0)
    cnt_ref[...] = jnp.broadcast_to(running, cnt_ref.shape).astype(jnp.int32)


def _router(x2, gain, mod3, shift_blk, scale_blk, tiles_per_batch, w_router, b_router, tm):
    n, d = x2.shape
    p_rows = d // 2 // V7X_LANES
    slot = jax.ShapeDtypeStruct((TOP_K, n), jnp.int32)
    slot_spec = pl.BlockSpec((TOP_K, tm), lambda i: (0, i))
    tri = jnp.triu(jnp.ones((tm, tm), BF16), k=1)
    return pl.pallas_call(
        functools.partial(_router_kernel, p_rows=p_rows),
        out_shape=(jax.ShapeDtypeStruct((n * p_rows, V7X_LANES), jnp.uint32),
                   slot, jax.ShapeDtypeStruct((TOP_K, n), F32), slot,
                   jax.ShapeDtypeStruct((N_EXPERTS, V7X_LANES), jnp.int32)),
        grid=(n // tm,),
        in_specs=[pl.BlockSpec((tm, d), lambda i: (i, 0)),
                  pl.BlockSpec((1, d), lambda i: (0, 0)),
                  pl.BlockSpec((None, 1, d), lambda i: (i // tiles_per_batch, 0, shift_blk)),
                  pl.BlockSpec((None, 1, d), lambda i: (i // tiles_per_batch, 0, scale_blk)),
                  pl.BlockSpec((N_EXPERTS, d), lambda i: (0, 0)),
                  pl.BlockSpec((N_EXPERTS, 1), lambda i: (0, 0)),
                  pl.BlockSpec((tm, tm), lambda i: (0, 0))],
        out_specs=(pl.BlockSpec((tm * p_rows, V7X_LANES), lambda i: (i, 0)),
                   slot_spec, slot_spec, slot_spec,
                   pl.BlockSpec((N_EXPERTS, V7X_LANES), lambda i: (0, 0))),
        scratch_shapes=[pltpu.VMEM((N_EXPERTS, 1), F32)],
        compiler_params=_cparams("arbitrary"),
        name="moe_router",
    )(x2, gain.reshape(1, d), mod3, mod3, w_router.T, b_router.reshape(-1, 1), tri)


def _slot_pos_kernel(pstart_ref, idx_ref, rank_ref, pos_ref):
    idx = idx_ref[...]
    pos = rank_ref[...]
    for e in range(N_EXPERTS):
        pos = pos + jnp.where(idx == e, pstart_ref[e], 0)
    pos_ref[...] = pos


def _slot_positions(pstart, idx_t, rank_t):
    k, n = idx_t.shape
    tn = next(t for t in (4096, 2048, 1024, 512) if n % t == 0)
    spec = pl.BlockSpec((k, tn), lambda i, ps: (0, i))
    return pl.pallas_call(
        _slot_pos_kernel,
        out_shape=jax.ShapeDtypeStruct((k, n), jnp.int32),
        grid_spec=pltpu.PrefetchScalarGridSpec(
            num_scalar_prefetch=1, grid=(n // tn,), in_specs=[spec, spec], out_specs=spec),
        compiler_params=_cparams("parallel"),
        name="moe_slot_positions",
    )(pstart, idx_t, rank_t)


def _start_row_dmas(idx_hbm_row, idx_smem, idx_sem, row_copy):
    cp = pltpu.make_async_copy(idx_hbm_row, idx_smem, idx_sem)
    cp.start()
    cp.wait()

    def issue(pair, c):
        r = 2 * pair
        row_copy(r, idx_smem[r]).start(priority=0)
        row_copy(r + 1, idx_smem[r + 1]).start(priority=1)
        return c

    lax.fori_loop(0, idx_smem.shape[0] // 2, issue, 0, unroll=4)


def _wait_row_dmas(n, row_copy):
    def drain(r, c):
        row_copy(r, 0).wait()
        return c

    lax.fori_loop(0, n, drain, 0, unroll=8)


def _dispatch_kernel(pos_hbm, hp_ref, xs_zero_hbm, xs_hbm, idx_smem, sem, idx_sem, *, p_rows):
    del xs_zero_hbm
    tt = hp_ref.shape[0] // p_rows

    def row_copy(r, dst_row):
        t = r & (tt - 1)
        src = hp_ref.at[pl.ds(pl.multiple_of(t * p_rows, p_rows), p_rows)]
        return pltpu.make_async_copy(src, xs_hbm.at[dst_row], sem)

    _start_row_dmas(pos_hbm.at[pl.program_id(0)], idx_smem, idx_sem, row_copy)
    _wait_row_dmas(idx_smem.shape[0], row_copy)


def _dispatch(pos_tiles, h_packed, n_rows, p_rows):
    n_tiles, slots = pos_tiles.shape
    tt = slots // TOP_K
    assert tt & (tt - 1) == 0
    xs_shape = jax.ShapeDtypeStruct((n_rows, p_rows, V7X_LANES), jnp.uint32)
    return pl.pallas_call(
        functools.partial(_dispatch_kernel, p_rows=p_rows),
        out_shape=xs_shape,
        grid=(n_tiles,),
        in_specs=[pl.BlockSpec(memory_space=pl.ANY),
                  pl.BlockSpec((tt * p_rows, V7X_LANES), lambda i: (i, 0)),
                  pl.BlockSpec(memory_space=pl.ANY)],
        out_specs=pl.BlockSpec(memory_space=pl.ANY),
        scratch_shapes=[pltpu.SMEM((slots,), jnp.int32),
                        pltpu.SemaphoreType.DMA,
                        pltpu.SemaphoreType.DMA],
        input_output_aliases={2: 0},
        compiler_params=_cparams("arbitrary"),
        name="moe_dispatch",
    )(pos_tiles, h_packed, jnp.zeros(xs_shape.shape, xs_shape.dtype))


def _experts_kernel(be_ref, nused_ref, xs_ref, wg_ref, wu_ref, bg_ref, bu_ref, wd_ref, bd_ref,
                    y_ref, x_scr, acc_scr, *, p_rows):
    i = pl.program_id(0)
    j = pl.program_id(1)
    nf = pl.num_programs(1)
    tb, d = x_scr.shape
    active = i < nused_ref[0]

    def hidden_chunk(x):
        a_gate = jnp.dot(x, wg_ref[...], preferred_element_type=F32) + bg_ref[...]
        a_up = jnp.dot(x, wu_ref[...], preferred_element_type=F32) + bu_ref[...]
        a_gate = jnp.minimum(a_gate, SWIGLU_LIMIT)
        a_up = jnp.clip(a_up, -SWIGLU_LIMIT, SWIGLU_LIMIT)
        act = (a_up + 1.0) * (a_gate * jax.nn.sigmoid(SWIGLU_ALPHA * a_gate))
        return jnp.dot(act.astype(BF16), wd_ref[...], preferred_element_type=F32)

    def store_rows(y):
        packed = _pack_bf16_pairs(y[:, :d // 2], y[:, d // 2:])
        for s in range(p_rows):
            y_ref[pl.ds(s, tb, stride=p_rows), :] = packed[:, s * V7X_LANES:(s + 1) * V7X_LANES]

    @pl.when(active & (j == 0))
    def _():
        for s in range(p_rows):
            lo, hi = _unpack_bf16_pairs(xs_ref[pl.ds(s, tb, stride=p_rows), :])
            x_scr[:, s * V7X_LANES:(s + 1) * V7X_LANES] = lo
            x_scr[:, d // 2 + s * V7X_LANES:d // 2 + (s + 1) * V7X_LANES] = hi
        acc_scr[...] = hidden_chunk(x_scr[...])

    @pl.when(active & (j > 0) & (j < nf - 1))
    def _():
        acc_scr[...] += hidden_chunk(x_scr[...])

    @pl.when(active & (j > 0) & (j == nf - 1))
    def _():
        store_rows(acc_scr[...] + hidden_chunk(x_scr[...]) + bd_ref[...])

    @pl.when(jnp.logical_not(active) & (j == nf - 1))
    def _():
        y_ref[...] = jnp.zeros_like(y_ref)


def _experts(block_expert, n_used, xs_rows, tb, w_gate_up, b_gate_up, w_down, b_down):
    e, d, two_f = w_gate_up.shape
    p_rows = d // 2 // V7X_LANES
    n_blocks = xs_rows.shape[0] // (tb * p_rows)
    f = two_f // 2
    tf = MOE_F_TILE
    nf = f // tf
    assert nf >= 2 and nf * tf == f
    row_spec = pl.BlockSpec((tb * p_rows, V7X_LANES), lambda i, j, be, nu: (i, 0))

    def fcol(i, j, nused):
        return jnp.where(i < nused[0], j, nf - 1)

    grid_spec = pltpu.PrefetchScalarGridSpec(
        num_scalar_prefetch=2,
        grid=(n_blocks, nf),
        in_specs=[row_spec,
                  pl.BlockSpec((None, d, tf), lambda i, j, be, nu: (be[i], 0, fcol(i, j, nu))),
                  pl.BlockSpec((None, d, tf), lambda i, j, be, nu: (be[i], 0, nf + fcol(i, j, nu))),
                  pl.BlockSpec((None, 1, tf), lambda i, j, be, nu: (be[i], 0, fcol(i, j, nu))),
                  pl.BlockSpec((None, 1, tf), lambda i, j, be, nu: (be[i], 0, nf + fcol(i, j, nu))),
                  pl.BlockSpec((None, tf, d), lambda i, j, be, nu: (be[i], fcol(i, j, nu), 0)),
                  pl.BlockSpec((None, 1, d), lambda i, j, be, nu: (be[i], 0, 0))],
        out_specs=row_spec,
        scratch_shapes=[pltpu.VMEM((tb, d), BF16),
                        pltpu.VMEM((tb, d), F32)])
    return pl.pallas_call(
        functools.partial(_experts_kernel, p_rows=p_rows),
        out_shape=jax.ShapeDtypeStruct(xs_rows.shape, jnp.uint32),
        grid_spec=grid_spec,
        compiler_params=_cparams("arbitrary", "arbitrary"),
        name="moe_experts",
    )(block_expert, n_used, xs_rows, w_gate_up, w_gate_up,
      b_gate_up.reshape(e, 1, two_f), b_gate_up.reshape(e, 1, two_f), w_down,
      b_down.reshape(e, 1, d))


def _combine_kernel(pos_hbm, y_hbm, x_ref, mgate_ref, rgate_ref, o_ref, idx_smem, gbuf_a, gbuf_b,
                    sem_a, sem_b, idx_sem, *, p_rows):
    i = pl.program_id(0)
    n_steps = pl.num_programs(0)
    tt, d = x_ref.shape
    n_rows = TOP_K * tt

    def row_copier(gbuf, sem):
        def row_copy(r, src_row):
            dst = gbuf.at[pl.ds(pl.multiple_of(r * p_rows, p_rows), p_rows)]
            return pltpu.make_async_copy(y_hbm.at[src_row], dst, sem)
        return row_copy

    def step(cur, nxt, first):
        if first:
            @pl.when(i == 0)
            def _():
                _start_row_dmas(pos_hbm.at[0], idx_smem, idx_sem, row_copier(*cur))

        @pl.when(i + 1 < n_steps)
        def _():
            _start_row_dmas(pos_hbm.at[i + 1], idx_smem, idx_sem, row_copier(*nxt))

        _wait_row_dmas(n_rows, row_copier(*cur))
        gbuf = cur[0]
        rg = rgate_ref[...]
        gates = [jnp.broadcast_to(rg[:, k:k + 1], (tt, V7X_LANES)) for k in range(TOP_K)]
        for s in range(p_rows):
            lo_tot = jnp.zeros((tt, V7X_LANES), F32)
            hi_tot = jnp.zeros((tt, V7X_LANES), F32)
            for k in range(TOP_K):
                lo, hi = _unpack_pairs_f32(gbuf[pl.ds(k * tt * p_rows + s, tt, stride=p_rows), :])
                lo_tot = lo_tot + gates[k] * lo
                hi_tot = hi_tot + gates[k] * hi
            for base, tot in ((0, lo_tot), (d // 2, hi_tot)):
                cols = slice(base + s * V7X_LANES, base + (s + 1) * V7X_LANES)
                o_ref[:, cols] = x_ref[:, cols] + mgate_ref[:, cols] * tot

    @pl.when(i % 2 == 0)
    def _():
        step((gbuf_a, sem_a), (gbuf_b, sem_b), True)

    @pl.when(i % 2 == 1)
    def _():
        step((gbuf_b, sem_b), (gbuf_a, sem_a), False)


def _combine(pos_tiles, y_rows, x2, mod3, gate_blk, router_gates, tiles_per_batch, tt):
    n, d = x2.shape
    p_rows = y_rows.shape[1]
    return pl.pallas_call(
        functools.partial(_combine_kernel, p_rows=p_rows),
        out_shape=jax.ShapeDtypeStruct((n, d), F32),
        grid=(n // tt,),
        in_specs=[pl.BlockSpec(memory_space=pl.ANY),
                  pl.BlockSpec(memory_space=pl.ANY),
                  pl.BlockSpec((tt, d), lambda i: (i, 0)),
                  pl.BlockSpec((None, 1, d), lambda i: (i // tiles_per_batch, 0, gate_blk)),
                  pl.BlockSpec((tt, TOP_K), lambda i: (i, 0))],
        out_specs=pl.BlockSpec((tt, d), lambda i: (i, 0)),
        scratch_shapes=[pltpu.SMEM((TOP_K * tt,), jnp.int32),
                        pltpu.VMEM((TOP_K * tt * p_rows, V7X_LANES), jnp.uint32),
                        pltpu.VMEM((TOP_K * tt * p_rows, V7X_LANES), jnp.uint32),
                        pltpu.SemaphoreType.DMA,
                        pltpu.SemaphoreType.DMA,
                        pltpu.SemaphoreType.DMA],
        compiler_params=_cparams("arbitrary"),
        name="moe_combine",
    )(pos_tiles, y_rows, x2, mod3, router_gates)


def _block_tables(counts, n_slots, tb):
    padded = (counts + tb - 1) // tb * tb
    pend = jnp.cumsum(padded)
    pstart = (pend - padded).astype(jnp.int32)
    n_blocks = -(-n_slots // tb) + N_EXPERTS
    block_start = jnp.arange(n_blocks, dtype=jnp.int32) * tb
    block_expert = jnp.minimum(
        jnp.sum(block_start[:, None] >= pend[None, :], axis=1), N_EXPERTS - 1).astype(jnp.int32)
    n_used = (pend[-1] // tb).astype(jnp.int32).reshape(1)
    return pstart, block_expert, n_used, n_blocks


def _moe(x2, norm2, mod3, tiles_per_batch_fn, w_router, b_router, w_gate_up, b_gate_up, w_down,
         b_down):
    n, d = x2.shape
    p_rows = d // 2 // V7X_LANES
    h_packed, idx_t, gate_t, rank_t, counts = _router(
        x2, norm2, mod3, 3, 4, tiles_per_batch_fn(ROW_TILE), w_router, b_router, ROW_TILE)
    pstart, block_expert, n_used, n_blocks = _block_tables(counts[:, 0], TOP_K * n, MOE_ROWS)
    pos = _slot_positions(pstart, idx_t, rank_t)
    tt = COMBINE_TOKENS
    pos_tiles = pos.reshape(TOP_K, n // tt, tt).transpose(1, 0, 2).reshape(n // tt, TOP_K * tt)
    xs = _dispatch(pos_tiles, h_packed, n_blocks * MOE_ROWS, p_rows)
    y_rows = _experts(block_expert, n_used, xs.reshape(-1, V7X_LANES), MOE_ROWS,
                      w_gate_up.astype(BF16), b_gate_up, w_down.astype(BF16), b_down)
    return _combine(pos_tiles, y_rows.reshape(-1, p_rows, V7X_LANES), x2, mod3, 5, gate_t.T,
                    tiles_per_batch_fn(tt), tt)


def _rope_swap_perm():
    half = MLA_ROPE // 2
    quarter = half // 2
    j = np.arange(MLA_ROPE)
    return (j // half) * half + (j % half + quarter) % half


def _rope_tables(n_tokens):
    t = jnp.arange(n_tokens, dtype=jnp.int32)
    row = (t // GRID_W).astype(F32)
    col = (t % GRID_W).astype(F32)
    n_freq = MLA_ROPE // 4
    inv_freq = ROPE_BASE ** (-jnp.arange(n_freq, dtype=F32) / n_freq)
    ar = row[:, None] * inv_freq
    ac = col[:, None] * inv_freq
    cos_t = jnp.concatenate([jnp.cos(ar), jnp.cos(ar), jnp.cos(ac), jnp.cos(ac)], axis=-1)
    sin_t = jnp.concatenate([-jnp.sin(ar), jnp.sin(ar), -jnp.sin(ac), jnp.sin(ac)], axis=-1)
    return cos_t, sin_t


def kernel(x, c, ctx, c_ctx, l0_w_mod, l0_b_mod, l0_norm1, l0_w_in, l0_mla_q_norm, l0_mla_w_q_up, l0_mla_kv_norm, l0_mla_w_kv_up, l0_mla_qk_q, l0_mla_qk_k, l0_na_qk_q, l0_na_qk_k, l0_na_rel_bias, l0_w_out, l0_norm2, l0_w_router, l0_b_router, l0_w_gate_up, l0_b_gate_up, l0_w_down, l0_b_down, l1_w_mod, l1_b_mod, l1_norm1, l1_w_in, l1_pool_w, l1_pool_scale, l1_conv_w, l1_w_out, l1_norm2, l1_w_router, l1_b_router, l1_w_gate_up, l1_b_gate_up, l1_w_down, l1_b_down):
    b, l, d = x.shape
    n_ctx = ctx.shape[1]
    n = b * l
    assert d % V7X_LANES == 0 and l % ROW_TILE == 0 and l % (NA_Q_ROWS * GRID_W) == 0
    assert l // GRID_W >= NA_K_ROWS and b + 1 <= V7X_SUBLANES
    x2 = x.reshape(n, d)

    def tiles_per_batch(tm):
        return l // tm

    c_rows = jnp.concatenate(
        [c, c_ctx[None], jnp.zeros((V7X_SUBLANES - b - 1, d), F32)], axis=0)
    perm = _rope_swap_perm()

    mod3 = _modulation(c_rows, l0_w_mod, l0_b_mod).reshape(V7X_SUBLANES, 1, 6 * d)
    kr0 = MLA_Q_RANK + MLA_KV_RANK
    kr1 = kr0 + MLA_ROPE
    w_in = jnp.concatenate(
        [l0_w_in[:, :kr1], l0_w_in[:, kr0:kr1][:, perm], l0_w_in[:, kr1:],
         jnp.zeros((d, U0_COLS - l0_w_in.shape[1] - MLA_ROPE), F32)], axis=1).astype(BF16)
    tpb = tiles_per_batch(ROW_TILE)
    u = _norm_proj(x2, l0_norm1, mod3, 0, 1, lambda i: i // tpb, w_in, ROW_TILE)
    ctx_tile = n_ctx if (b * n_ctx) % ROW_TILE else ROW_TILE
    u_ctx = _norm_proj(ctx.reshape(b * n_ctx, d), l0_norm1, mod3, 0, 1, lambda i: b, w_in, ctx_tile)

    wq = l0_mla_w_q_up.reshape(MLA_Q_RANK, MLA_HEADS, MLA_QK)
    wq = jnp.concatenate([wq, wq[:, :, MLA_NOPE:][:, :, perm]], axis=-1)
    wq = wq.reshape(MLA_Q_RANK, -1).astype(BF16)
    gq = jnp.concatenate([l0_mla_qk_q, l0_mla_qk_q[MLA_NOPE:][perm]]).reshape(1, -1)
    gk = jnp.concatenate([l0_mla_qk_k, l0_mla_qk_k[MLA_NOPE:][perm]]).reshape(1, -1)
    wkv = l0_mla_w_kv_up.astype(BF16)
    cos_t, sin_t = _rope_tables(l)
    cos_k = jnp.concatenate([cos_t, jnp.ones((n_ctx, MLA_ROPE), F32)], axis=0)
    sin_k = jnp.concatenate([sin_t, jnp.zeros((n_ctx, MLA_ROPE), F32)], axis=0)
    q = _q_prep(u, b, l, l0_mla_q_norm, wq, gq, cos_t, sin_t, ROW_TILE)
    u3 = u.reshape(b, l, U0_COLS)
    uctx3 = u_ctx.reshape(b, n_ctx, U0_COLS)
    ckv0 = U0_CKV_BLK * V7X_LANES
    ckv1 = ckv0 + MLA_KV_RANK + 2 * MLA_ROPE
    ckv = jnp.concatenate([u3[:, :, ckv0:ckv1], uctx3[:, :, ckv0:ckv1]], axis=1)
    k_m, v_m = _kv_prep(ckv, l0_mla_kv_norm, wkv, gk, cos_k, sin_k, n_ctx)
    expert_w = (l0_w_gate_up, l0_w_down, l1_w_gate_up, l1_w_down)
    o_m, expert_w_bf16 = _mla_attention(
        q, k_m, v_m, ROW_TILE, [w.reshape(-1, w.shape[-1]) for w in expert_w])
    l0_w_gate_up, l0_w_down, l1_w_gate_up, l1_w_down = (
        wb.reshape(w.shape) for wb, w in zip(expert_w_bf16, expert_w))

    bias = _na_bias_table(l0_na_rel_bias, l // GRID_W)
    o_n = _na_attention(u3, uctx3, l0_na_qk_q, l0_na_qk_k, bias)

    x2 = _attn_out(o_m.reshape(n, -1), o_n.reshape(n, -1), l0_w_out.astype(BF16), x2, mod3, 2,
                   tpb, ROW_TILE)
    x2 = _moe(x2, l0_norm2, mod3, tiles_per_batch, l0_w_router, l0_b_router, l0_w_gate_up,
              l0_b_gate_up, l0_w_down, l0_b_down)

    mod3 = _modulation(c_rows, l1_w_mod, l1_b_mod).reshape(V7X_SUBLANES, 1, 6 * d)
    u = _norm_proj(x2, l1_norm1, mod3, 0, 1, lambda i: i // tpb, l1_w_in.astype(BF16), ROW_TILE)
    x2 = _pool_conv_mixer(u, l1_pool_w, l1_pool_scale, l1_conv_w, l1_w_out.astype(BF16), x2, mod3,
                          2, l, ROW_TILE)
    x2 = _moe(x2, l1_norm2, mod3, tiles_per_batch, l1_w_router, l1_b_router, l1_w_gate_up,
              l1_b_gate_up, l1_w_down, l1_b_down)
    return x2.reshape(b, l, d)
```

```python
import functools

import numpy as np
import jax
import jax.numpy as jnp
from jax import lax
from jax.experimental import pallas as pl
from jax.experimental.pallas import tpu as pltpu

F32 = jnp.float32
BF16 = jnp.bfloat16

V7X_LANES = 128
V7X_SUBLANES = 8
V7X_VMEM_BYTES = 64 * 2**20
VMEM_LIMIT_BYTES = V7X_VMEM_BYTES * 7 // 8

GRID_W = 64
NORM_EPS = 1e-6
ROPE_BASE = 10000.0
MLA_HEADS = 8
MLA_Q_RANK = 512
MLA_KV_RANK = 256
MLA_NOPE = 128
MLA_ROPE = 64
MLA_V = 128
MLA_QK = MLA_NOPE + MLA_ROPE
MLA_QK_PAD = 256
MLA_PV_TILE = 256
NA_HEADS = 8
NA_DIM = 128
NA_WIN_ROWS = 8
NA_WIN_COLS = 16
NA_Q_ROWS = 8
NA_K_ROWS = 16
POOL_WINDOWS = (2, 4, 8, 16)
POOL_WIDTH = 1024
POOL_GROUP = POOL_WIDTH // len(POOL_WINDOWS)
CONV_CH = 1024
HALO = 16
N_EXPERTS = 32
TOP_K = 4
SWIGLU_LIMIT = 7.0
SWIGLU_ALPHA = 1.702
NEG_BIG = -1e30
LOG2_E = 1.4426950408889634

U0_COLS = 4096
U0_CKV_BLK = 4
U0_NAQ_BLK = 7
U0_NAK_BLK = 15
U0_NAV_BLK = 23

ROW_TILE = 512
MOE_ROWS = 512
MOE_F_TILE = 1024
COMBINE_TOKENS = 256


def _cparams(*sem):
    return pltpu.CompilerParams(dimension_semantics=sem, vmem_limit_bytes=VMEM_LIMIT_BYTES)


def _rms(x, gain):
    ms = jnp.mean(x * x, axis=-1, keepdims=True)
    return x * lax.rsqrt(ms + NORM_EPS) * gain


def _dot_nt(a, b):
    return lax.dot_general(a, b, (((1,), (1,)), ((), ())), preferred_element_type=F32)


def _mod_kernel(c_ref, w_ref, b_ref, o_ref):
    c = c_ref[...]
    s = c * jax.nn.sigmoid(c)
    o_ref[...] = jnp.dot(s, w_ref[...], preferred_element_type=F32,
                         precision=lax.Precision.HIGHEST) + b_ref[...]


def _modulation(c_rows, w_mod, b_mod):
    d, n = w_mod.shape
    tn = 512
    assert n % tn == 0
    return pl.pallas_call(
        _mod_kernel,
        out_shape=jax.ShapeDtypeStruct((V7X_SUBLANES, n), F32),
        grid=(n // tn,),
        in_specs=[pl.BlockSpec((V7X_SUBLANES, d), lambda j: (0, 0)),
                  pl.BlockSpec((d, tn), lambda j: (0, j)),
                  pl.BlockSpec((1, tn), lambda j: (0, j))],
        out_specs=pl.BlockSpec((V7X_SUBLANES, tn), lambda j: (0, j)),
        compiler_params=_cparams("parallel"),
        name="modulation",
    )(c_rows, w_mod, b_mod.reshape(1, n))


def _norm_proj_kernel(x_ref, g_ref, sh_ref, sc_ref, w_ref, o_ref, *, tn):
    h = (_rms(x_ref[...], g_ref[...]) * (1.0 + sc_ref[...]) + sh_ref[...]).astype(BF16)
    for c in range(0, w_ref.shape[1], tn):
        o_ref[:, c:c + tn] = jnp.dot(
            h, w_ref[:, c:c + tn], preferred_element_type=F32).astype(o_ref.dtype)


def _norm_proj(x2, gain, mod3, shift_blk, scale_blk, group_of_tile, w_bf16, tm):
    n, d = x2.shape
    ncol = w_bf16.shape[1]
    return pl.pallas_call(
        functools.partial(_norm_proj_kernel, tn=1024),
        out_shape=jax.ShapeDtypeStruct((n, ncol), BF16),
        grid=(n // tm,),
        in_specs=[pl.BlockSpec((tm, d), lambda i: (i, 0)),
                  pl.BlockSpec((1, d), lambda i: (0, 0)),
                  pl.BlockSpec((None, 1, d), lambda i: (group_of_tile(i), 0, shift_blk)),
                  pl.BlockSpec((None, 1, d), lambda i: (group_of_tile(i), 0, scale_blk)),
                  pl.BlockSpec((d, ncol), lambda i: (0, 0), pipeline_mode=pl.Buffered(1))],
        out_specs=pl.BlockSpec((tm, ncol), lambda i: (i, 0)),
        compiler_params=_cparams("parallel"),
        name="norm_proj",
    )(x2, gain.reshape(1, d), mod3, mod3, w_bf16)


def _lane_iota(shape):
    return lax.broadcasted_iota(jnp.int32, shape, len(shape) - 1)


def _q_prep_kernel(cq_ref, qn_ref, wq_ref, g_ref, cos_ref, sin_ref, o_ref):
    cn = _rms(cq_ref[...].astype(F32), qn_ref[...]).astype(BF16)
    t_all = jnp.dot(cn, wq_ref[...], preferred_element_type=F32)
    in_head = _lane_iota((t_all.shape[0], MLA_QK_PAD)) < MLA_QK
    g = g_ref[...]
    cos = cos_ref[...]
    sin = sin_ref[...]
    for h in range(MLA_HEADS):
        t = t_all[:, h * MLA_QK_PAD:(h + 1) * MLA_QK_PAD]
        sq = jnp.where(in_head, t * t, 0.0)
        r = lax.rsqrt(jnp.sum(sq, axis=-1, keepdims=True) * (1.0 / MLA_QK) + NORM_EPS)
        tn = t * r * g
        rope = tn[:, MLA_NOPE:MLA_QK] * cos + tn[:, MLA_QK:] * sin
        out = jnp.concatenate([tn[:, :MLA_NOPE], rope, jnp.zeros_like(rope)], axis=-1)
        o_ref[h] = (out * (MLA_QK ** -0.5 * LOG2_E)).astype(o_ref.dtype)


def _q_prep(u, b, l, q_norm, wq, gq, cos_t, sin_t, tm):
    nt = l // tm
    return pl.pallas_call(
        _q_prep_kernel,
        out_shape=jax.ShapeDtypeStruct((b, MLA_HEADS, l, MLA_QK_PAD), BF16),
        grid=(b, nt),
        in_specs=[pl.BlockSpec((tm, MLA_Q_RANK), lambda bi, i: (bi * nt + i, 0)),
                  pl.BlockSpec((1, MLA_Q_RANK), lambda bi, i: (0, 0)),
                  pl.BlockSpec((MLA_Q_RANK, MLA_HEADS * MLA_QK_PAD), lambda bi, i: (0, 0)),
                  pl.BlockSpec((1, MLA_QK_PAD), lambda bi, i: (0, 0)),
                  pl.BlockSpec((tm, MLA_ROPE), lambda bi, i: (i, 0)),
                  pl.BlockSpec((tm, MLA_ROPE), lambda bi, i: (i, 0))],
        out_specs=pl.BlockSpec((None, MLA_HEADS, tm, MLA_QK_PAD), lambda bi, i: (bi, 0, i, 0)),
        compiler_params=_cparams("parallel", "parallel"),
        name="mla_q_prep",
    )(u, q_norm.reshape(1, -1), wq, gq, cos_t, sin_t)


def _kv_prep_kernel(ckv_ref, kvn_ref, wkv_ref, g_ref, cos_ref, sin_ref, k_ref, v_ref):
    cn = _rms(ckv_ref[:, :MLA_KV_RANK].astype(F32), kvn_ref[...]).astype(BF16)
    t_all = jnp.dot(cn, wkv_ref[...], preferred_element_type=F32)
    kr = ckv_ref[:, MLA_KV_RANK:].astype(F32)
    kr_ss = jnp.sum(jnp.where(_lane_iota(kr.shape) < MLA_ROPE, kr * kr, 0.0), axis=-1, keepdims=True)
    g = g_ref[...]
    cos = cos_ref[...]
    sin = sin_ref[...]
    head_w = MLA_NOPE + MLA_V
    for h in range(MLA_HEADS):
        kn = t_all[:, h * head_w:h * head_w + MLA_NOPE]
        ss = jnp.sum(kn * kn, axis=-1, keepdims=True) + kr_ss
        r = lax.rsqrt(ss * (1.0 / MLA_QK) + NORM_EPS)
        kn_n = kn * r * g[:, :MLA_NOPE]
        kr_n = kr * r * g[:, MLA_NOPE:]
        rope = kr_n[:, :MLA_ROPE] * cos + kr_n[:, MLA_ROPE:] * sin
        k_ref[h] = jnp.concatenate([kn_n, rope, jnp.zeros_like(rope)], axis=-1).astype(k_ref.dtype)
        v_ref[h] = t_all[:, h * head_w + MLA_NOPE:(h + 1) * head_w].astype(v_ref.dtype)


def _kv_prep(ckv, kv_norm, wkv, gk, cos_t, sin_t, tm):
    b, m, w = ckv.shape
    return pl.pallas_call(
        _kv_prep_kernel,
        out_shape=(jax.ShapeDtypeStruct((b, MLA_HEADS, m, MLA_QK_PAD), BF16),
                   jax.ShapeDtypeStruct((b, MLA_HEADS, m, MLA_V), BF16)),
        grid=(b, m // tm),
        in_specs=[pl.BlockSpec((None, tm, w), lambda bi, i: (bi, i, 0)),
                  pl.BlockSpec((1, MLA_KV_RANK), lambda bi, i: (0, 0)),
                  pl.BlockSpec(wkv.shape, lambda bi, i: (0, 0)),
                  pl.BlockSpec((1, MLA_QK_PAD), lambda bi, i: (0, 0)),
                  pl.BlockSpec((tm, MLA_ROPE), lambda bi, i: (i, 0)),
                  pl.BlockSpec((tm, MLA_ROPE), lambda bi, i: (i, 0))],
        out_specs=(pl.BlockSpec((None, MLA_HEADS, tm, MLA_QK_PAD), lambda bi, i: (bi, 0, i, 0)),
                   pl.BlockSpec((None, MLA_HEADS, tm, MLA_V), lambda bi, i: (bi, 0, i, 0))),
        compiler_params=_cparams("parallel", "parallel"),
        name="mla_kv_prep",
    )(ckv, kv_norm.reshape(1, -1), wkv, gk, cos_t, sin_t)


def _mla_kernel(*refs, tk, nk, n_cast):
    q_ref, k_ref, v_ref = refs[:3]
    cast_in = refs[3:3 + n_cast]
    o_ref = refs[3 + n_cast]
    cast_out = refs[4 + n_cast:4 + 2 * n_cast]
    sa_scr, sb_scr, m_scr, l_scr, acc_scr = refs[4 + 2 * n_cast:]
    tq = q_ref.shape[0]

    for w_ref, wb_ref in zip(cast_in, cast_out):
        wb_ref[...] = w_ref[...].astype(BF16)

    m_scr[...] = jnp.full_like(m_scr, NEG_BIG)
    l_scr[...] = jnp.zeros_like(l_scr)
    acc_scr[...] = jnp.zeros_like(acc_scr)

    def scores(c, dst):
        off = pl.multiple_of(c * tk, tk)
        dst[...] = _dot_nt(q_ref[...], k_ref[pl.ds(off, tk), :])

    def consume(c, src):
        off = pl.multiple_of(c * tk, tk)
        m_prev = m_scr[...]
        m_new = jnp.maximum(m_prev, jnp.max(src[...], axis=-1, keepdims=True))
        alpha = jnp.exp2(m_prev - m_new)
        psum = jnp.zeros((tq, 1), F32)
        pv = jnp.zeros((tq, MLA_V), F32)
        for j in range(0, tk, MLA_PV_TILE):
            p = jnp.exp2((src[:, j:j + MLA_PV_TILE] - m_new).astype(BF16))
            psum = psum + jnp.sum(p.astype(F32), axis=-1, keepdims=True)
            pv = pv + jnp.dot(p, v_ref[pl.ds(off + j, MLA_PV_TILE), :], preferred_element_type=F32)
        l_scr[...] = alpha * l_scr[...] + psum
        acc_scr[...] = alpha * acc_scr[...] + pv
        m_scr[...] = m_new

    scores(0, sa_scr)

    def body(pair, carry):
        c = 2 * pair
        scores(c + 1, sb_scr)
        consume(c, sa_scr)
        scores(c + 2, sa_scr)
        consume(c + 1, sb_scr)
        return carry

    lax.fori_loop(0, (nk - 1) // 2, body, 0)
    if nk % 2 == 1:
        consume(nk - 1, sa_scr)
    else:
        scores(nk - 1, sb_scr)
        consume(nk - 2, sa_scr)
        consume(nk - 1, sb_scr)
    o_ref[...] = (acc_scr[...] / l_scr[...]).astype(o_ref.dtype)


def _mla_attention(q, k, v, tq, cast_slabs):
    b, h, l, _ = q.shape
    m = k.shape[2]
    nq = l // tq
    steps = b * h * nq
    tk = next(t for t in (1280, 1024, 768, 512, 256, 128) if m % t == 0)
    assert tk % MLA_PV_TILE == 0

    def slab_spec(w):
        rows = w.shape[0] // steps
        assert rows * steps == w.shape[0] and rows % (2 * V7X_SUBLANES) == 0
        return pl.BlockSpec((rows, w.shape[1]), lambda bi, hi, i: ((bi * h + hi) * nq + i, 0))

    slab_specs = [slab_spec(w) for w in cast_slabs]
    resident = dict(pipeline_mode=pl.Buffered(1))
    out = pl.pallas_call(
        functools.partial(_mla_kernel, tk=tk, nk=m // tk, n_cast=len(cast_slabs)),
        out_shape=[jax.ShapeDtypeStruct((b, l, h * MLA_V), BF16)]
        + [jax.ShapeDtypeStruct(w.shape, BF16) for w in cast_slabs],
        grid=(b, h, nq),
        in_specs=[pl.BlockSpec((None, None, tq, MLA_QK_PAD), lambda bi, hi, i: (bi, hi, i, 0)),
                  pl.BlockSpec((None, None, m, MLA_QK_PAD), lambda bi, hi, i: (bi, hi, 0, 0),
                               **resident),
                  pl.BlockSpec((None, None, m, MLA_V), lambda bi, hi, i: (bi, hi, 0, 0),
                               **resident)] + slab_specs,
        out_specs=[pl.BlockSpec((None, tq, MLA_V), lambda bi, hi, i: (bi, i, hi))] + slab_specs,
        scratch_shapes=[pltpu.VMEM((tq, tk), F32), pltpu.VMEM((tq, tk), F32),
                        pltpu.VMEM((tq, 1), F32), pltpu.VMEM((tq, 1), F32),
                        pltpu.VMEM((tq, MLA_V), F32)],
        compiler_params=_cparams("parallel", "parallel", "arbitrary"),
        name="mla_attention",
    )(q, k, v, *cast_slabs)
    return out[0], out[1:]


def _na_kernel(q_ref, k_ref, v_ref, kc_ref, vc_ref, gq_ref, gk_ref, bias_ref, o_ref, *, rows):
    rb = pl.program_id(2)
    k_row0 = jnp.clip(rb * NA_Q_ROWS - NA_WIN_ROWS // 2, 0, rows - NA_K_ROWS)
    off = pl.multiple_of(k_row0 * GRID_W, NA_WIN_ROWS // 2 * GRID_W)
    nk = NA_K_ROWS * GRID_W
    q = (_rms(q_ref[...].astype(F32), gq_ref[...]) * (NA_DIM ** -0.5)).astype(BF16)
    kw = _rms(k_ref[pl.ds(off, nk), :].astype(F32), gk_ref[...]).astype(BF16)
    kc = _rms(kc_ref[...].astype(F32), gk_ref[...]).astype(BF16)
    s_w = _dot_nt(q, kw) + bias_ref[...]
    s_c = _dot_nt(q, kc)
    m = jnp.maximum(jnp.max(s_w, axis=-1, keepdims=True), jnp.max(s_c, axis=-1, keepdims=True))
    p_w = jnp.exp(s_w - m)
    p_c = jnp.exp(s_c - m)
    denom = jnp.sum(p_w, axis=-1, keepdims=True) + jnp.sum(p_c, axis=-1, keepdims=True)
    o = (jnp.dot(p_w.astype(BF16), v_ref[pl.ds(off, nk), :], preferred_element_type=F32)
         + jnp.dot(p_c.astype(BF16), vc_ref[...], preferred_element_type=F32))
    o_ref[...] = (o / denom).astype(o_ref.dtype)


def _na_bias_table(rel_bias, rows):
    nrb = rows // NA_Q_ROWS
    tables = []
    for rb in (0, min(1, nrb - 1), nrb - 1):
        k_row0 = int(np.clip(rb * NA_Q_ROWS - NA_WIN_ROWS // 2, 0, rows - NA_K_ROWS))
        r = rb * NA_Q_ROWS + np.arange(NA_Q_ROWS)
        kr = k_row0 + np.arange(NA_K_ROWS)
        qc = np.arange(GRID_W)
        kc = np.arange(GRID_W)
        rs = np.clip(r - NA_WIN_ROWS // 2, 0, rows - NA_WIN_ROWS)
        cs = np.clip(qc - NA_WIN_COLS // 2, 0, GRID_W - NA_WIN_COLS)
        row_ok = (kr[None, :] >= rs[:, None]) & (kr[None, :] < rs[:, None] + NA_WIN_ROWS)
        col_ok = (kc[None, :] >= cs[:, None]) & (kc[None, :] < cs[:, None] + NA_WIN_COLS)
        ri = np.clip(kr[None, :] - r[:, None] + NA_WIN_ROWS - 1, 0, 2 * NA_WIN_ROWS - 2)
        ci = np.clip(kc[None, :] - qc[:, None] + NA_WIN_COLS - 1, 0, 2 * NA_WIN_COLS - 2)
        sel_r = (ri.reshape(-1)[:, None] == np.arange(2 * NA_WIN_ROWS - 1)[None, :]).astype(np.float32)
        sel_c = (np.arange(2 * NA_WIN_COLS - 1)[:, None] == ci.reshape(-1)[None, :]).astype(np.float32)
        bias = jnp.einsum("rm,hmc->hrc", sel_r, rel_bias.astype(F32), precision=lax.Precision.HIGHEST)
        bias = jnp.einsum("hrc,cq->hrq", bias, sel_c, precision=lax.Precision.HIGHEST)
        bias = bias.reshape(NA_HEADS, NA_Q_ROWS, NA_K_ROWS, GRID_W, GRID_W).transpose(0, 1, 3, 2, 4)
        ok = row_ok[:, None, :, None] & col_ok[None, :, None, :]
        bias = jnp.where(ok[None], bias.astype(F32), NEG_BIG)
        tables.append(bias.reshape(NA_HEADS, NA_Q_ROWS * GRID_W, NA_K_ROWS * GRID_W))
    return jnp.stack(tables)


def _na_attention(u3, uctx3, gq, gk, bias):
    b, l, _ = u3.shape
    n_ctx = uctx3.shape[1]
    rows = l // GRID_W
    nrb = rows // NA_Q_ROWS
    tq = NA_Q_ROWS * GRID_W

    def pattern(rb):
        return jnp.where(rb == 0, 0, jnp.where(rb == nrb - 1, 2, 1))

    return pl.pallas_call(
        functools.partial(_na_kernel, rows=rows),
        out_shape=jax.ShapeDtypeStruct((b, l, NA_HEADS * NA_DIM), BF16),
        grid=(b, NA_HEADS, nrb),
        in_specs=[pl.BlockSpec((None, tq, NA_DIM), lambda bi, h, rb: (bi, rb, U0_NAQ_BLK + h)),
                  pl.BlockSpec((None, l, NA_DIM), lambda bi, h, rb: (bi, 0, U0_NAK_BLK + h)),
                  pl.BlockSpec((None, l, NA_DIM), lambda bi, h, rb: (bi, 0, U0_NAV_BLK + h)),
                  pl.BlockSpec((None, n_ctx, NA_DIM), lambda bi, h, rb: (bi, 0, U0_NAK_BLK + h)),
                  pl.BlockSpec((None, n_ctx, NA_DIM), lambda bi, h, rb: (bi, 0, U0_NAV_BLK + h)),
                  pl.BlockSpec((1, NA_DIM), lambda bi, h, rb: (0, 0)),
                  pl.BlockSpec((1, NA_DIM), lambda bi, h, rb: (0, 0)),
                  pl.BlockSpec((None, None, tq, NA_K_ROWS * GRID_W),
                               lambda bi, h, rb: (pattern(rb), h, 0, 0))],
        out_specs=pl.BlockSpec((None, tq, NA_DIM), lambda bi, h, rb: (bi, rb, h)),
        compiler_params=_cparams("parallel", "parallel", "arbitrary"),
        name="neighbourhood_attention",
    )(u3, u3, u3, uctx3, uctx3, gq.reshape(1, -1), gk.reshape(1, -1), bias)


def _attn_out_kernel(am_ref, an_ref, wm_ref, wn_ref, x_ref, gate_ref, o_ref):
    y = (jnp.dot(am_ref[...], wm_ref[...], preferred_element_type=F32)
         + jnp.dot(an_ref[...], wn_ref[...], preferred_element_type=F32))
    o_ref[...] = x_ref[...] + gate_ref[...] * y


def _attn_out(a_m, a_n, w_out_bf16, x2, mod3, gate_blk, tiles_per_batch, tm):
    n, d = x2.shape
    half = a_m.shape[1]
    return pl.pallas_call(
        _attn_out_kernel,
        out_shape=jax.ShapeDtypeStruct((n, d), F32),
        grid=(n // tm,),
        in_specs=[pl.BlockSpec((tm, half), lambda i: (i, 0)),
                  pl.BlockSpec((tm, half), lambda i: (i, 0)),
                  pl.BlockSpec((half, d), lambda i: (0, 0)),
                  pl.BlockSpec((half, d), lambda i: (1, 0)),
                  pl.BlockSpec((tm, d), lambda i: (i, 0)),
                  pl.BlockSpec((None, 1, d), lambda i: (i // tiles_per_batch, 0, gate_blk))],
        out_specs=pl.BlockSpec((tm, d), lambda i: (i, 0)),
        compiler_params=_cparams("parallel"),
        name="attn_out_residual",
    )(a_m, a_n, w_out_bf16, w_out_bf16, x2, mod3)


def _pool_conv_kernel(prev_ref, u_ref, next_ref, pw_ref, ps_ref, cw_ref, wp_ref, wc_ref, x_ref,
                      gate_ref, o_ref, *, seq_len, tiles_per_batch):
    tm = u_ref.shape[0]
    n_ext = tm + 2 * HALO
    pos0 = (pl.program_id(0) % tiles_per_batch) * tm
    pos_ext = pos0 - HALO + lax.broadcasted_iota(jnp.int32, (n_ext, 1), 0)
    valid = (pos_ext >= 0) & (pos_ext < seq_len)

    def ext(c0, c1):
        e = jnp.concatenate([prev_ref[:, c0:c1], u_ref[:, c0:c1], next_ref[:, c0:c1]], axis=0)
        return jnp.where(valid, e.astype(F32), 0.0)

    def shifted(a, d):
        return pltpu.roll(a, (-d) % n_ext, axis=0)

    pos = (pos0 + lax.broadcasted_iota(jnp.int32, (tm, 1), 0)).astype(F32)
    y = jnp.zeros((tm, x_ref.shape[1]), F32)
    for g, w in enumerate(POOL_WINDOWS):
        c0 = g * POOL_GROUP
        e = ext(c0, c0 + POOL_GROUP)
        acc = e + shifted(e, -1)
        span = 1
        while 2 * span < w:
            acc = shifted(acc, span) + shifted(acc, -span)
            span *= 2
        half = w // 2
        cnt = jnp.minimum(pos + half, float(seq_len)) - jnp.maximum(pos - half, 0.0)
        pooled = acc[HALO:HALO + tm] / cnt - e[HALO:HALO + tm]
        mixed = jnp.dot(pooled.astype(BF16), pw_ref[g], preferred_element_type=F32)
        y_g = (mixed * ps_ref[:, c0:c0 + POOL_GROUP]).astype(BF16)
        y = y + jnp.dot(y_g, wp_ref[c0:c0 + POOL_GROUP, :], preferred_element_type=F32)

    gb0 = POOL_WIDTH
    gc0 = POOL_WIDTH + CONV_CH
    v0 = POOL_WIDTH + 2 * CONV_CH
    z = ext(gc0, gc0 + CONV_CH) * ext(v0, v0 + CONV_CH)
    cw = cw_ref[...]
    conv = cw[0:1] * shifted(z, -1) + cw[1:2] * z + cw[2:3] * shifted(z, 1)
    y_conv = u_ref[:, gb0:gb0 + CONV_CH].astype(F32) * conv[HALO:HALO + tm]
    y = y + jnp.dot(y_conv.astype(BF16), wc_ref[...], preferred_element_type=F32)
    o_ref[...] = x_ref[...] + gate_ref[...] * y


def _pool_conv_mixer(u, pool_w, pool_scale, conv_w, w_out_bf16, x2, mod3, gate_blk, seq_len, tm):
    n, d = x2.shape
    width = u.shape[1]
    tiles_per_batch = seq_len // tm
    hb = tm // HALO
    last = n // HALO - 1
    return pl.pallas_call(
        functools.partial(_pool_conv_kernel, seq_len=seq_len, tiles_per_batch=tiles_per_batch),
        out_shape=jax.ShapeDtypeStruct((n, d), F32),
        grid=(n // tm,),
        in_specs=[pl.BlockSpec((HALO, width), lambda i: (jnp.maximum(i * hb - 1, 0), 0)),
                  pl.BlockSpec((tm, width), lambda i: (i, 0)),
                  pl.BlockSpec((HALO, width), lambda i: (jnp.minimum((i + 1) * hb, last), 0)),
                  pl.BlockSpec(pool_w.shape, lambda i: (0, 0, 0)),
                  pl.BlockSpec((1, POOL_WIDTH), lambda i: (0, 0)),
                  pl.BlockSpec((V7X_SUBLANES, CONV_CH), lambda i: (0, 0)),
                  pl.BlockSpec((POOL_WIDTH, d), lambda i: (0, 0)),
                  pl.BlockSpec((CONV_CH, d), lambda i: (1, 0)),
                  pl.BlockSpec((tm, d), lambda i: (i, 0)),
                  pl.BlockSpec((None, 1, d), lambda i: (i // tiles_per_batch, 0, gate_blk))],
        out_specs=pl.BlockSpec((tm, d), lambda i: (i, 0)),
        compiler_params=_cparams("parallel"),
        name="pool_conv_mixer",
    )(u, u, u, pool_w.astype(BF16), pool_scale.reshape(1, -1),
      jnp.pad(conv_w, ((0, V7X_SUBLANES - conv_w.shape[0]), (0, 0))), w_out_bf16, w_out_bf16, x2, mod3)


def _split_bf16(a):
    hi = a.astype(BF16)
    return hi, (a - hi.astype(F32)).astype(BF16)


def _pack_bf16_pairs(lo, hi):
    lo_bits = lax.bitcast_convert_type(lo.astype(BF16).astype(F32), jnp.uint32) >> 16
    hi_bits = lax.bitcast_convert_type(hi.astype(BF16).astype(F32), jnp.uint32) & jnp.uint32(0xFFFF0000)
    return hi_bits | lo_bits


def _unpack_pairs_f32(w):
    lo = lax.bitcast_convert_type(w << 16, F32)
    hi = lax.bitcast_convert_type(w & jnp.uint32(0xFFFF0000), F32)
    return lo, hi


def _unpack_bf16_pairs(w):
    lo, hi = _unpack_pairs_f32(w)
    return lo.astype(BF16), hi.astype(BF16)


def _router_kernel(x_ref, g_ref, sh_ref, sc_ref, wr_ref, br_ref, tri_ref, hp_ref, idx_ref, gate_ref,
                   rank_ref, cnt_ref, run_scr, *, p_rows):
    tm, d = x_ref.shape

    @pl.when(pl.program_id(0) == 0)
    def _():
        run_scr[...] = jnp.zeros_like(run_scr)

    h = _rms(x_ref[...], g_ref[...]) * (1.0 + sc_ref[...]) + sh_ref[...]
    packed = _pack_bf16_pairs(h[:, :d // 2], h[:, d // 2:])
    for s in range(p_rows):
        hp_ref[pl.ds(s, tm, stride=p_rows), :] = packed[:, s * V7X_LANES:(s + 1) * V7X_LANES]
    h_hi, h_lo = _split_bf16(h)
    w_hi, w_lo = _split_bf16(wr_ref[...])
    logits = _dot_nt(w_hi, h_hi) + _dot_nt(w_hi, h_lo) + _dot_nt(w_lo, h_hi) + br_ref[...]
    expert = lax.broadcasted_iota(jnp.int32, logits.shape, 0)
    vals, ids = [], []
    for _ in range(TOP_K):
        mx = jnp.max(logits, axis=0, keepdims=True)
        am = jnp.min(jnp.where(logits == mx, expert, N_EXPERTS), axis=0, keepdims=True)
        vals.append(mx)
        ids.append(am)
        logits = jnp.where(expert == am, -jnp.inf, logits)
    ex = [jnp.exp(v - vals[0]) for v in vals]
    tot = ex[0] + ex[1] + ex[2] + ex[3]
    idx_ref[...] = jnp.concatenate(ids, axis=0)
    gate_ref[...] = jnp.concatenate([e / tot for e in ex], axis=0)

    running = run_scr[...]
    ranks = []
    for k in range(TOP_K):
        hit = expert == ids[k]
        onehot = jnp.where(hit, 1.0, 0.0)
        before = jnp.dot(onehot.astype(BF16), tri_ref[...], preferred_element_type=F32)
        rank = jnp.sum(jnp.where(hit, before + running, 0.0), axis=0, keepdims=True)
        ranks.append(rank.astype(jnp.int32))
        running = running + jnp.sum(onehot, axis=1, keepdims=True)
    run_scr[...] = running
    rank_ref[...] = jnp.concatenate(ranks, axis=0)
    cnt_ref[...] = jnp.broadcast_to(running, cnt_ref.shape).astype(jnp.int32)


def _router(x2, gain, mod3, shift_blk, scale_blk, tiles_per_batch, w_router, b_router, tm):
    n, d = x2.shape
    p_rows = d // 2 // V7X_LANES
    slot = jax.ShapeDtypeStruct((TOP_K, n), jnp.int32)
    slot_spec = pl.BlockSpec((TOP_K, tm), lambda i: (0, i))
    tri = jnp.triu(jnp.ones((tm, tm), BF16), k=1)
    return pl.pallas_call(
        functools.partial(_router_kernel, p_rows=p_rows),
        out_shape=(jax.ShapeDtypeStruct((n * p_rows, V7X_LANES), jnp.uint32),
                   slot, jax.ShapeDtypeStruct((TOP_K, n), F32), slot,
                   jax.ShapeDtypeStruct((N_EXPERTS, V7X_LANES), jnp.int32)),
        grid=(n // tm,),
        in_specs=[pl.BlockSpec((tm, d), lambda i: (i, 0)),
                  pl.BlockSpec((1, d), lambda i: (0, 0)),
                  pl.BlockSpec((None, 1, d), lambda i: (i // tiles_per_batch, 0, shift_blk)),
                  pl.BlockSpec((None, 1, d), lambda i: (i // tiles_per_batch, 0, scale_blk)),
                  pl.BlockSpec((N_EXPERTS, d), lambda i: (0, 0)),
                  pl.BlockSpec((N_EXPERTS, 1), lambda i: (0, 0)),
                  pl.BlockSpec((tm, tm), lambda i: (0, 0))],
        out_specs=(pl.BlockSpec((tm * p_rows, V7X_LANES), lambda i: (i, 0)),
                   slot_spec, slot_spec, slot_spec,
                   pl.BlockSpec((N_EXPERTS, V7X_LANES), lambda i: (0, 0))),
        scratch_shapes=[pltpu.VMEM((N_EXPERTS, 1), F32)],
        compiler_params=_cparams("arbitrary"),
        name="moe_router",
    )(x2, gain.reshape(1, d), mod3, mod3, w_router.T, b_router.reshape(-1, 1), tri)


def _slot_pos_kernel(pstart_ref, idx_ref, rank_ref, pos_ref):
    idx = idx_ref[...]
    pos = rank_ref[...]
    for e in range(N_EXPERTS):
        pos = pos + jnp.where(idx == e, pstart_ref[e], 0)
    pos_ref[...] = pos


def _slot_positions(pstart, idx_t, rank_t):
    k, n = idx_t.shape
    tn = next(t for t in (4096, 2048, 1024, 512) if n % t == 0)
    spec = pl.BlockSpec((k, tn), lambda i, ps: (0, i))
    return pl.pallas_call(
        _slot_pos_kernel,
        out_shape=jax.ShapeDtypeStruct((k, n), jnp.int32),
        grid_spec=pltpu.PrefetchScalarGridSpec(
            num_scalar_prefetch=1, grid=(n // tn,), in_specs=[spec, spec], out_specs=spec),
        compiler_params=_cparams("parallel"),
        name="moe_slot_positions",
    )(pstart, idx_t, rank_t)


def _start_row_dmas(idx_hbm_row, idx_smem, idx_sem, row_copy):
    cp = pltpu.make_async_copy(idx_hbm_row, idx_smem, idx_sem)
    cp.start()
    cp.wait()

    def issue(pair, c):
        r = 2 * pair
        row_copy(r, idx_smem[r]).start(priority=0)
        row_copy(r + 1, idx_smem[r + 1]).start(priority=1)
        return c

    lax.fori_loop(0, idx_smem.shape[0] // 2, issue, 0, unroll=4)


def _wait_row_dmas(n, row_copy):
    def drain(r, c):
        row_copy(r, 0).wait()
        return c

    lax.fori_loop(0, n, drain, 0, unroll=8)


def _dispatch_kernel(pos_hbm, hp_ref, xs_zero_hbm, xs_hbm, idx_smem, sem, idx_sem, *, p_rows):
    del xs_zero_hbm
    tt = hp_ref.shape[0] // p_rows

    def row_copy(r, dst_row):
        t = r & (tt - 1)
        src = hp_ref.at[pl.ds(pl.multiple_of(t * p_rows, p_rows), p_rows)]
        return pltpu.make_async_copy(src, xs_hbm.at[dst_row], sem)

    _start_row_dmas(pos_hbm.at[pl.program_id(0)], idx_smem, idx_sem, row_copy)
    _wait_row_dmas(idx_smem.shape[0], row_copy)


def _dispatch(pos_tiles, h_packed, n_rows, p_rows, row_buffer):
    n_tiles, slots = pos_tiles.shape
    tt = slots // TOP_K
    assert tt & (tt - 1) == 0
    xs_shape = jax.ShapeDtypeStruct((n_rows, p_rows, V7X_LANES), jnp.uint32)
    if row_buffer is None:
        row_buffer = jnp.zeros(xs_shape.shape, xs_shape.dtype)
    assert row_buffer.shape == xs_shape.shape and row_buffer.dtype == xs_shape.dtype
    return pl.pallas_call(
        functools.partial(_dispatch_kernel, p_rows=p_rows),
        out_shape=xs_shape,
        grid=(n_tiles,),
        in_specs=[pl.BlockSpec(memory_space=pl.ANY),
                  pl.BlockSpec((tt * p_rows, V7X_LANES), lambda i: (i, 0)),
                  pl.BlockSpec(memory_space=pl.ANY)],
        out_specs=pl.BlockSpec(memory_space=pl.ANY),
        scratch_shapes=[pltpu.SMEM((slots,), jnp.int32),
                        pltpu.SemaphoreType.DMA,
                        pltpu.SemaphoreType.DMA],
        input_output_aliases={2: 0},
        compiler_params=_cparams("arbitrary"),
        name="moe_dispatch",
    )(pos_tiles, h_packed, row_buffer)


def _experts_kernel(be_ref, nused_ref, xs_ref, wg_ref, wu_ref, bg_ref, bu_ref, wd_ref, bd_ref,
                    y_ref, x_scr, acc_scr, *, p_rows):
    i = pl.program_id(0)
    j = pl.program_id(1)
    nf = pl.num_programs(1)
    tb, d = x_scr.shape
    active = i < nused_ref[0]

    def hidden_chunk(x):
        a_gate = jnp.dot(x, wg_ref[...], preferred_element_type=F32) + bg_ref[...]
        a_up = jnp.dot(x, wu_ref[...], preferred_element_type=F32) + bu_ref[...]
        a_gate = jnp.minimum(a_gate, SWIGLU_LIMIT)
        a_up = jnp.clip(a_up, -SWIGLU_LIMIT, SWIGLU_LIMIT)
        act = (a_up + 1.0) * (a_gate * jax.nn.sigmoid(SWIGLU_ALPHA * a_gate))
        return jnp.dot(act.astype(BF16), wd_ref[...], preferred_element_type=F32)

    def store_rows(y):
        packed = _pack_bf16_pairs(y[:, :d // 2], y[:, d // 2:])
        for s in range(p_rows):
            y_ref[pl.ds(s, tb, stride=p_rows), :] = packed[:, s * V7X_LANES:(s + 1) * V7X_LANES]

    @pl.when(active & (j == 0))
    def _():
        for s in range(p_rows):
            lo, hi = _unpack_bf16_pairs(xs_ref[pl.ds(s, tb, stride=p_rows), :])
            x_scr[:, s * V7X_LANES:(s + 1) * V7X_LANES] = lo
            x_scr[:, d // 2 + s * V7X_LANES:d // 2 + (s + 1) * V7X_LANES] = hi
        acc_scr[...] = hidden_chunk(x_scr[...])

    @pl.when(active & (j > 0) & (j < nf - 1))
    def _():
        acc_scr[...] += hidden_chunk(x_scr[...])

    @pl.when(active & (j > 0) & (j == nf - 1))
    def _():
        store_rows(acc_scr[...] + hidden_chunk(x_scr[...]) + bd_ref[...])

    @pl.when(jnp.logical_not(active) & (j == nf - 1))
    def _():
        y_ref[...] = jnp.zeros_like(y_ref)


def _experts(block_expert, n_used, xs_rows, tb, w_gate_up, b_gate_up, w_down, b_down):
    e, d, two_f = w_gate_up.shape
    p_rows = d // 2 // V7X_LANES
    n_blocks = xs_rows.shape[0] // (tb * p_rows)
    f = two_f // 2
    tf = MOE_F_TILE
    nf = f // tf
    assert nf >= 2 and nf * tf == f
    row_spec = pl.BlockSpec((tb * p_rows, V7X_LANES), lambda i, j, be, nu: (i, 0))

    def fcol(i, j, nused):
        return jnp.where(i < nused[0], j, nf - 1)

    grid_spec = pltpu.PrefetchScalarGridSpec(
        num_scalar_prefetch=2,
        grid=(n_blocks, nf),
        in_specs=[row_spec,
                  pl.BlockSpec((None, d, tf), lambda i, j, be, nu: (be[i], 0, fcol(i, j, nu))),
                  pl.BlockSpec((None, d, tf), lambda i, j, be, nu: (be[i], 0, nf + fcol(i, j, nu))),
                  pl.BlockSpec((None, 1, tf), lambda i, j, be, nu: (be[i], 0, fcol(i, j, nu))),
                  pl.BlockSpec((None, 1, tf), lambda i, j, be, nu: (be[i], 0, nf + fcol(i, j, nu))),
                  pl.BlockSpec((None, tf, d), lambda i, j, be, nu: (be[i], fcol(i, j, nu), 0)),
                  pl.BlockSpec((None, 1, d), lambda i, j, be, nu: (be[i], 0, 0))],
        out_specs=row_spec,
        scratch_shapes=[pltpu.VMEM((tb, d), BF16),
                        pltpu.VMEM((tb, d), F32)])
    return pl.pallas_call(
        functools.partial(_experts_kernel, p_rows=p_rows),
        out_shape=jax.ShapeDtypeStruct(xs_rows.shape, jnp.uint32),
        grid_spec=grid_spec,
        compiler_params=_cparams("arbitrary", "arbitrary"),
        name="moe_experts",
    )(block_expert, n_used, xs_rows, w_gate_up, w_gate_up,
      b_gate_up.reshape(e, 1, two_f), b_gate_up.reshape(e, 1, two_f), w_down,
      b_down.reshape(e, 1, d))


def _combine_kernel(pos_hbm, y_hbm, x_ref, mgate_ref, rgate_ref, o_ref, idx_smem, gbuf_a, gbuf_b,
                    sem_a, sem_b, idx_sem, *, p_rows):
    i = pl.program_id(0)
    n_steps = pl.num_programs(0)
    tt, d = x_ref.shape
    n_rows = TOP_K * tt

    def row_copier(gbuf, sem):
        def row_copy(r, src_row):
            dst = gbuf.at[pl.ds(pl.multiple_of(r * p_rows, p_rows), p_rows)]
            return pltpu.make_async_copy(y_hbm.at[src_row], dst, sem)
        return row_copy

    def step(cur, nxt, first):
        if first:
            @pl.when(i == 0)
            def _():
                _start_row_dmas(pos_hbm.at[0], idx_smem, idx_sem, row_copier(*cur))

        @pl.when(i + 1 < n_steps)
        def _():
            _start_row_dmas(pos_hbm.at[i + 1], idx_smem, idx_sem, row_copier(*nxt))

        _wait_row_dmas(n_rows, row_copier(*cur))
        gbuf = cur[0]
        rg = rgate_ref[...]
        gates = [jnp.broadcast_to(rg[:, k:k + 1], (tt, V7X_LANES)) for k in range(TOP_K)]
        for s in range(p_rows):
            lo_tot = jnp.zeros((tt, V7X_LANES), F32)
            hi_tot = jnp.zeros((tt, V7X_LANES), F32)
            for k in range(TOP_K):
                lo, hi = _unpack_pairs_f32(gbuf[pl.ds(k * tt * p_rows + s, tt, stride=p_rows), :])
                lo_tot = lo_tot + gates[k] * lo
                hi_tot = hi_tot + gates[k] * hi
            for base, tot in ((0, lo_tot), (d // 2, hi_tot)):
                cols = slice(base + s * V7X_LANES, base + (s + 1) * V7X_LANES)
                o_ref[:, cols] = x_ref[:, cols] + mgate_ref[:, cols] * tot

    @pl.when(i % 2 == 0)
    def _():
        step((gbuf_a, sem_a), (gbuf_b, sem_b), True)

    @pl.when(i % 2 == 1)
    def _():
        step((gbuf_b, sem_b), (gbuf_a, sem_a), False)


def _combine(pos_tiles, y_rows, x2, mod3, gate_blk, router_gates, tiles_per_batch, tt):
    n, d = x2.shape
    p_rows = y_rows.shape[1]
    return pl.pallas_call(
        functools.partial(_combine_kernel, p_rows=p_rows),
        out_shape=jax.ShapeDtypeStruct((n, d), F32),
        grid=(n // tt,),
        in_specs=[pl.BlockSpec(memory_space=pl.ANY),
                  pl.BlockSpec(memory_space=pl.ANY),
                  pl.BlockSpec((tt, d), lambda i: (i, 0)),
                  pl.BlockSpec((None, 1, d), lambda i: (i // tiles_per_batch, 0, gate_blk)),
                  pl.BlockSpec((tt, TOP_K), lambda i: (i, 0))],
        out_specs=pl.BlockSpec((tt, d), lambda i: (i, 0)),
        scratch_shapes=[pltpu.SMEM((TOP_K * tt,), jnp.int32),
                        pltpu.VMEM((TOP_K * tt * p_rows, V7X_LANES), jnp.uint32),
                        pltpu.VMEM((TOP_K * tt * p_rows, V7X_LANES), jnp.uint32),
                        pltpu.SemaphoreType.DMA,
                        pltpu.SemaphoreType.DMA,
                        pltpu.SemaphoreType.DMA],
        compiler_params=_cparams("arbitrary"),
        name="moe_combine",
    )(pos_tiles, y_rows, x2, mod3, router_gates)


def _block_tables(counts, n_slots, tb):
    padded = (counts + tb - 1) // tb * tb
    pend = jnp.cumsum(padded)
    pstart = (pend - padded).astype(jnp.int32)
    n_blocks = -(-n_slots // tb) + N_EXPERTS
    block_start = jnp.arange(n_blocks, dtype=jnp.int32) * tb
    block_expert = jnp.minimum(
        jnp.sum(block_start[:, None] >= pend[None, :], axis=1), N_EXPERTS - 1).astype(jnp.int32)
    n_used = (pend[-1] // tb).astype(jnp.int32).reshape(1)
    return pstart, block_expert, n_used, n_blocks


def _moe(x2, norm2, mod3, tiles_per_batch_fn, w_router, b_router, w_gate_up, b_gate_up, w_down,
         b_down, row_buffer=None):
    n, d = x2.shape
    p_rows = d // 2 // V7X_LANES
    h_packed, idx_t, gate_t, rank_t, counts = _router(
        x2, norm2, mod3, 3, 4, tiles_per_batch_fn(ROW_TILE), w_router, b_router, ROW_TILE)
    pstart, block_expert, n_used, n_blocks = _block_tables(counts[:, 0], TOP_K * n, MOE_ROWS)
    pos = _slot_positions(pstart, idx_t, rank_t)
    tt = COMBINE_TOKENS
    pos_tiles = pos.reshape(TOP_K, n // tt, tt).transpose(1, 0, 2).reshape(n // tt, TOP_K * tt)
    xs = _dispatch(pos_tiles, h_packed, n_blocks * MOE_ROWS, p_rows, row_buffer)
    y_rows = _experts(block_expert, n_used, xs.reshape(-1, V7X_LANES), MOE_ROWS,
                      w_gate_up.astype(BF16), b_gate_up, w_down.astype(BF16), b_down)
    out = _combine(pos_tiles, y_rows.reshape(-1, p_rows, V7X_LANES), x2, mod3, 5, gate_t.T,
                   tiles_per_batch_fn(tt), tt)
    return out, xs


def _rope_swap_perm():
    half = MLA_ROPE // 2
    quarter = half // 2
    j = np.arange(MLA_ROPE)
    return (j // half) * half + (j % half + quarter) % half


def _rope_tables(n_tokens):
    t = jnp.arange(n_tokens, dtype=jnp.int32)
    row = (t // GRID_W).astype(F32)
    col = (t % GRID_W).astype(F32)
    n_freq = MLA_ROPE // 4
    inv_freq = ROPE_BASE ** (-jnp.arange(n_freq, dtype=F32) / n_freq)
    ar = row[:, None] * inv_freq
    ac = col[:, None] * inv_freq
    cos_t = jnp.concatenate([jnp.cos(ar), jnp.cos(ar), jnp.cos(ac), jnp.cos(ac)], axis=-1)
    sin_t = jnp.concatenate([-jnp.sin(ar), jnp.sin(ar), -jnp.sin(ac), jnp.sin(ac)], axis=-1)
    return cos_t, sin_t


def kernel(x, c, ctx, c_ctx, l0_w_mod, l0_b_mod, l0_norm1, l0_w_in, l0_mla_q_norm, l0_mla_w_q_up, l0_mla_kv_norm, l0_mla_w_kv_up, l0_mla_qk_q, l0_mla_qk_k, l0_na_qk_q, l0_na_qk_k, l0_na_rel_bias, l0_w_out, l0_norm2, l0_w_router, l0_b_router, l0_w_gate_up, l0_b_gate_up, l0_w_down, l0_b_down, l1_w_mod, l1_b_mod, l1_norm1, l1_w_in, l1_pool_w, l1_pool_scale, l1_conv_w, l1_w_out, l1_norm2, l1_w_router, l1_b_router, l1_w_gate_up, l1_b_gate_up, l1_w_down, l1_b_down):
    b, l, d = x.shape
    n_ctx = ctx.shape[1]
    n = b * l
    assert d % V7X_LANES == 0 and l % ROW_TILE == 0 and l % (NA_Q_ROWS * GRID_W) == 0
    assert l // GRID_W >= NA_K_ROWS and b + 1 <= V7X_SUBLANES
    x2 = x.reshape(n, d)

    def tiles_per_batch(tm):
        return l // tm

    c_rows = jnp.concatenate(
        [c, c_ctx[None], jnp.zeros((V7X_SUBLANES - b - 1, d), F32)], axis=0)
    perm = _rope_swap_perm()

    mod3 = _modulation(c_rows, l0_w_mod, l0_b_mod).reshape(V7X_SUBLANES, 1, 6 * d)
    kr0 = MLA_Q_RANK + MLA_KV_RANK
    kr1 = kr0 + MLA_ROPE
    w_in = jnp.concatenate(
        [l0_w_in[:, :kr1], l0_w_in[:, kr0:kr1][:, perm], l0_w_in[:, kr1:],
         jnp.zeros((d, U0_COLS - l0_w_in.shape[1] - MLA_ROPE), F32)], axis=1).astype(BF16)
    tpb = tiles_per_batch(ROW_TILE)
    u = _norm_proj(x2, l0_norm1, mod3, 0, 1, lambda i: i // tpb, w_in, ROW_TILE)
    ctx_tile = n_ctx if (b * n_ctx) % ROW_TILE else ROW_TILE
    u_ctx = _norm_proj(ctx.reshape(b * n_ctx, d), l0_norm1, mod3, 0, 1, lambda i: b, w_in, ctx_tile)

    wq = l0_mla_w_q_up.reshape(MLA_Q_RANK, MLA_HEADS, MLA_QK)
    wq = jnp.concatenate([wq, wq[:, :, MLA_NOPE:][:, :, perm]], axis=-1)
    wq = wq.reshape(MLA_Q_RANK, -1).astype(BF16)
    gq = jnp.concatenate([l0_mla_qk_q, l0_mla_qk_q[MLA_NOPE:][perm]]).reshape(1, -1)
    gk = jnp.concatenate([l0_mla_qk_k, l0_mla_qk_k[MLA_NOPE:][perm]]).reshape(1, -1)
    wkv = l0_mla_w_kv_up.astype(BF16)
    cos_t, sin_t = _rope_tables(l)
    cos_k = jnp.concatenate([cos_t, jnp.ones((n_ctx, MLA_ROPE), F32)], axis=0)
    sin_k = jnp.concatenate([sin_t, jnp.zeros((n_ctx, MLA_ROPE), F32)], axis=0)
    q = _q_prep(u, b, l, l0_mla_q_norm, wq, gq, cos_t, sin_t, ROW_TILE)
    u3 = u.reshape(b, l, U0_COLS)
    uctx3 = u_ctx.reshape(b, n_ctx, U0_COLS)
    ckv0 = U0_CKV_BLK * V7X_LANES
    ckv1 = ckv0 + MLA_KV_RANK + 2 * MLA_ROPE
    ckv = jnp.concatenate([u3[:, :, ckv0:ckv1], uctx3[:, :, ckv0:ckv1]], axis=1)
    k_m, v_m = _kv_prep(ckv, l0_mla_kv_norm, wkv, gk, cos_k, sin_k, n_ctx)
    expert_w = (l0_w_gate_up, l0_w_down, l1_w_gate_up, l1_w_down)
    o_m, expert_w_bf16 = _mla_attention(
        q, k_m, v_m, ROW_TILE, [w.reshape(-1, w.shape[-1]) for w in expert_w])
    l0_w_gate_up, l0_w_down, l1_w_gate_up, l1_w_down = (
        wb.reshape(w.shape) for wb, w in zip(expert_w_bf16, expert_w))

    bias = _na_bias_table(l0_na_rel_bias, l // GRID_W)
    o_n = _na_attention(u3, uctx3, l0_na_qk_q, l0_na_qk_k, bias)

    x2 = _attn_out(o_m.reshape(n, -1), o_n.reshape(n, -1), l0_w_out.astype(BF16), x2, mod3, 2,
                   tpb, ROW_TILE)
    x2, expert_rows = _moe(x2, l0_norm2, mod3, tiles_per_batch, l0_w_router, l0_b_router,
                           l0_w_gate_up, l0_b_gate_up, l0_w_down, l0_b_down)

    mod3 = _modulation(c_rows, l1_w_mod, l1_b_mod).reshape(V7X_SUBLANES, 1, 6 * d)
    u = _norm_proj(x2, l1_norm1, mod3, 0, 1, lambda i: i // tpb, l1_w_in.astype(BF16), ROW_TILE)
    x2 = _pool_conv_mixer(u, l1_pool_w, l1_pool_scale, l1_conv_w, l1_w_out.astype(BF16), x2, mod3,
                          2, l, ROW_TILE)
    x2, _ = _moe(x2, l1_norm2, mod3, tiles_per_batch, l1_w_router, l1_b_router, l1_w_gate_up,
                 l1_b_gate_up, l1_w_down, l1_b_down, row_buffer=expert_rows)
    return x2.reshape(b, l, d)
```

```python
import functools

import numpy as np
import jax
import jax.numpy as jnp
from jax import lax
from jax.experimental import pallas as pl
from jax.experimental.pallas import tpu as pltpu

F32 = jnp.float32
BF16 = jnp.bfloat16

V7X_LANES = 128
V7X_SUBLANES = 8
V7X_VMEM_BYTES = 64 * 2**20
VMEM_LIMIT_BYTES = V7X_VMEM_BYTES * 7 // 8

GRID_W = 64
NORM_EPS = 1e-6
ROPE_BASE = 10000.0
MLA_HEADS = 8
MLA_Q_RANK = 512
MLA_KV_RANK = 256
MLA_NOPE = 128
MLA_ROPE = 64
MLA_V = 128
MLA_QK = MLA_NOPE + MLA_ROPE
MLA_QK_PAD = 256
MLA_PV_TILE = 256
NA_HEADS = 8
NA_DIM = 128
NA_WIN_ROWS = 8
NA_WIN_COLS = 16
NA_Q_ROWS = 8
NA_K_ROWS = 16
POOL_WINDOWS = (2, 4, 8, 16)
POOL_WIDTH = 1024
POOL_GROUP = POOL_WIDTH // len(POOL_WINDOWS)
CONV_CH = 1024
HALO = 16
N_EXPERTS = 32
TOP_K = 4
SWIGLU_LIMIT = 7.0
SWIGLU_ALPHA = 1.702
NEG_BIG = -1e30
LOG2_E = 1.4426950408889634

U0_COLS = 4096
U0_CKV_BLK = 4
U0_NAQ_BLK = 7
U0_NAK_BLK = 15
U0_NAV_BLK = 23

ROW_TILE = 512
MOE_ROWS = 512
MOE_F_TILE = 1024
COMBINE_TOKENS = 256


def _cparams(*sem):
    return pltpu.CompilerParams(dimension_semantics=sem, vmem_limit_bytes=VMEM_LIMIT_BYTES)


def _rms(x, gain):
    ms = jnp.mean(x * x, axis=-1, keepdims=True)
    return x * lax.rsqrt(ms + NORM_EPS) * gain


def _dot_nt(a, b):
    return lax.dot_general(a, b, (((1,), (1,)), ((), ())), preferred_element_type=F32)


def _mod_kernel(c_ref, w_ref, b_ref, o_ref):
    c = c_ref[...]
    s = c * jax.nn.sigmoid(c)
    o_ref[...] = jnp.dot(s, w_ref[...], preferred_element_type=F32,
                         precision=lax.Precision.HIGHEST) + b_ref[...]


def _modulation(c_rows, w_mod, b_mod):
    d, n = w_mod.shape
    tn = 512
    assert n % tn == 0
    return pl.pallas_call(
        _mod_kernel,
        out_shape=jax.ShapeDtypeStruct((V7X_SUBLANES, n), F32),
        grid=(n // tn,),
        in_specs=[pl.BlockSpec((V7X_SUBLANES, d), lambda j: (0, 0)),
                  pl.BlockSpec((d, tn), lambda j: (0, j)),
                  pl.BlockSpec((1, tn), lambda j: (0, j))],
        out_specs=pl.BlockSpec((V7X_SUBLANES, tn), lambda j: (0, j)),
        compiler_params=_cparams("parallel"),
        name="modulation",
    )(c_rows, w_mod, b_mod.reshape(1, n))


def _norm_proj_kernel(x_ref, g_ref, sh_ref, sc_ref, w_ref, o_ref, *, tn):
    h = (_rms(x_ref[...], g_ref[...]) * (1.0 + sc_ref[...]) + sh_ref[...]).astype(BF16)
    for c in range(0, w_ref.shape[1], tn):
        o_ref[:, c:c + tn] = jnp.dot(
            h, w_ref[:, c:c + tn], preferred_element_type=F32).astype(o_ref.dtype)


def _norm_proj(x2, gain, mod3, shift_blk, scale_blk, group_of_tile, w_bf16, tm):
    n, d = x2.shape
    ncol = w_bf16.shape[1]
    return pl.pallas_call(
        functools.partial(_norm_proj_kernel, tn=1024),
        out_shape=jax.ShapeDtypeStruct((n, ncol), BF16),
        grid=(n // tm,),
        in_specs=[pl.BlockSpec((tm, d), lambda i: (i, 0)),
                  pl.BlockSpec((1, d), lambda i: (0, 0)),
                  pl.BlockSpec((None, 1, d), lambda i: (group_of_tile(i), 0, shift_blk)),
                  pl.BlockSpec((None, 1, d), lambda i: (group_of_tile(i), 0, scale_blk)),
                  pl.BlockSpec((d, ncol), lambda i: (0, 0), pipeline_mode=pl.Buffered(1))],
        out_specs=pl.BlockSpec((tm, ncol), lambda i: (i, 0)),
        compiler_params=_cparams("parallel"),
        name="norm_proj",
    )(x2, gain.reshape(1, d), mod3, mod3, w_bf16)


def _lane_iota(shape):
    return lax.broadcasted_iota(jnp.int32, shape, len(shape) - 1)


def _q_prep_kernel(cq_ref, qn_ref, wq_ref, g_ref, cos_ref, sin_ref, o_ref):
    cn = _rms(cq_ref[...].astype(F32), qn_ref[...]).astype(BF16)
    t_all = jnp.dot(cn, wq_ref[...], preferred_element_type=F32)
    in_head = _lane_iota((t_all.shape[0], MLA_QK_PAD)) < MLA_QK
    g = g_ref[...]
    cos = cos_ref[...]
    sin = sin_ref[...]
    for h in range(MLA_HEADS):
        t = t_all[:, h * MLA_QK_PAD:(h + 1) * MLA_QK_PAD]
        sq = jnp.where(in_head, t * t, 0.0)
        r = lax.rsqrt(jnp.sum(sq, axis=-1, keepdims=True) * (1.0 / MLA_QK) + NORM_EPS)
        tn = t * r * g
        rope = tn[:, MLA_NOPE:MLA_QK] * cos + tn[:, MLA_QK:] * sin
        out = jnp.concatenate([tn[:, :MLA_NOPE], rope, jnp.zeros_like(rope)], axis=-1)
        o_ref[h] = (out * (MLA_QK ** -0.5 * LOG2_E)).astype(o_ref.dtype)


def _q_prep(u, b, l, q_norm, wq, gq, cos_t, sin_t, tm):
    nt = l // tm
    return pl.pallas_call(
        _q_prep_kernel,
        out_shape=jax.ShapeDtypeStruct((b, MLA_HEADS, l, MLA_QK_PAD), BF16),
        grid=(b, nt),
        in_specs=[pl.BlockSpec((tm, MLA_Q_RANK), lambda bi, i: (bi * nt + i, 0)),
                  pl.BlockSpec((1, MLA_Q_RANK), lambda bi, i: (0, 0)),
                  pl.BlockSpec((MLA_Q_RANK, MLA_HEADS * MLA_QK_PAD), lambda bi, i: (0, 0)),
                  pl.BlockSpec((1, MLA_QK_PAD), lambda bi, i: (0, 0)),
                  pl.BlockSpec((tm, MLA_ROPE), lambda bi, i: (i, 0)),
                  pl.BlockSpec((tm, MLA_ROPE), lambda bi, i: (i, 0))],
        out_specs=pl.BlockSpec((None, MLA_HEADS, tm, MLA_QK_PAD), lambda bi, i: (bi, 0, i, 0)),
        compiler_params=_cparams("parallel", "parallel"),
        name="mla_q_prep",
    )(u, q_norm.reshape(1, -1), wq, gq, cos_t, sin_t)


def _kv_prep_kernel(ckv_ref, kvn_ref, wkv_ref, g_ref, cos_ref, sin_ref, k_ref, v_ref):
    cn = _rms(ckv_ref[:, :MLA_KV_RANK].astype(F32), kvn_ref[...]).astype(BF16)
    t_all = jnp.dot(cn, wkv_ref[...], preferred_element_type=F32)
    kr = ckv_ref[:, MLA_KV_RANK:].astype(F32)
    kr_ss = jnp.sum(jnp.where(_lane_iota(kr.shape) < MLA_ROPE, kr * kr, 0.0), axis=-1, keepdims=True)
    g = g_ref[...]
    cos = cos_ref[...]
    sin = sin_ref[...]
    head_w = MLA_NOPE + MLA_V
    for h in range(MLA_HEADS):
        kn = t_all[:, h * head_w:h * head_w + MLA_NOPE]
        ss = jnp.sum(kn * kn, axis=-1, keepdims=True) + kr_ss
        r = lax.rsqrt(ss * (1.0 / MLA_QK) + NORM_EPS)
        kn_n = kn * r * g[:, :MLA_NOPE]
        kr_n = kr * r * g[:, MLA_NOPE:]
        rope = kr_n[:, :MLA_ROPE] * cos + kr_n[:, MLA_ROPE:] * sin
        k_ref[h] = jnp.concatenate([kn_n, rope, jnp.zeros_like(rope)], axis=-1).astype(k_ref.dtype)
        v_ref[h] = t_all[:, h * head_w + MLA_NOPE:(h + 1) * head_w].astype(v_ref.dtype)


def _kv_prep(ckv, kv_norm, wkv, gk, cos_t, sin_t, tm):
    b, m, w = ckv.shape
    return pl.pallas_call(
        _kv_prep_kernel,
        out_shape=(jax.ShapeDtypeStruct((b, MLA_HEADS, m, MLA_QK_PAD), BF16),
                   jax.ShapeDtypeStruct((b, MLA_HEADS, m, MLA_V), BF16)),
        grid=(b, m // tm),
        in_specs=[pl.BlockSpec((None, tm, w), lambda bi, i: (bi, i, 0)),
                  pl.BlockSpec((1, MLA_KV_RANK), lambda bi, i: (0, 0)),
                  pl.BlockSpec(wkv.shape, lambda bi, i: (0, 0)),
                  pl.BlockSpec((1, MLA_QK_PAD), lambda bi, i: (0, 0)),
                  pl.BlockSpec((tm, MLA_ROPE), lambda bi, i: (i, 0)),
                  pl.BlockSpec((tm, MLA_ROPE), lambda bi, i: (i, 0))],
        out_specs=(pl.BlockSpec((None, MLA_HEADS, tm, MLA_QK_PAD), lambda bi, i: (bi, 0, i, 0)),
                   pl.BlockSpec((None, MLA_HEADS, tm, MLA_V), lambda bi, i: (bi, 0, i, 0))),
        compiler_params=_cparams("parallel", "parallel"),
        name="mla_kv_prep",
    )(ckv, kv_norm.reshape(1, -1), wkv, gk, cos_t, sin_t)


def _mla_kernel(*refs, tk, nk, n_cast):
    q_ref, qn_ref, k_ref, v_ref = refs[:4]
    cast_in = refs[4:4 + n_cast]
    o_ref = refs[4 + n_cast]
    cast_out = refs[5 + n_cast:5 + 2 * n_cast]
    s0_scr, sa_scr, sb_scr, m_scr, l_scr, acc_scr = refs[5 + 2 * n_cast:]
    tq = q_ref.shape[0]

    def scores(q, c, dst):
        off = pl.multiple_of(c * tk, tk)
        dst[...] = _dot_nt(q[...], k_ref[pl.ds(off, tk), :])

    @pl.when(pl.program_id(2) == 0)
    def _():
        scores(q_ref, 0, s0_scr)

    for w_ref, wb_ref in zip(cast_in, cast_out):
        wb_ref[...] = w_ref[...].astype(BF16)

    m_scr[...] = jnp.full_like(m_scr, NEG_BIG)
    l_scr[...] = jnp.zeros_like(l_scr)
    acc_scr[...] = jnp.zeros_like(acc_scr)

    def consume(c, src):
        off = pl.multiple_of(c * tk, tk)
        m_prev = m_scr[...]
        m_new = jnp.maximum(m_prev, jnp.max(src[...], axis=-1, keepdims=True))
        alpha = jnp.exp2(m_prev - m_new)
        psum = jnp.zeros((tq, 1), F32)
        pv = jnp.zeros((tq, MLA_V), F32)
        for j in range(0, tk, MLA_PV_TILE):
            p = jnp.exp2((src[:, j:j + MLA_PV_TILE] - m_new).astype(BF16))
            psum = psum + jnp.sum(p.astype(F32), axis=-1, keepdims=True)
            pv = pv + jnp.dot(p, v_ref[pl.ds(off + j, MLA_PV_TILE), :], preferred_element_type=F32)
        l_scr[...] = alpha * l_scr[...] + psum
        acc_scr[...] = alpha * acc_scr[...] + pv
        m_scr[...] = m_new

    scores(q_ref, 1, sa_scr)
    consume(0, s0_scr)

    def body(pair, carry):
        c = 2 * pair + 1
        scores(q_ref, c + 1, sb_scr)
        consume(c, sa_scr)
        scores(q_ref, c + 2, sa_scr)
        consume(c + 1, sb_scr)
        return carry

    lax.fori_loop(0, (nk - 2) // 2, body, 0)
    if nk % 2 == 0:
        scores(qn_ref, 0, s0_scr)
        consume(nk - 1, sa_scr)
    else:
        scores(q_ref, nk - 1, sb_scr)
        consume(nk - 2, sa_scr)
        scores(qn_ref, 0, s0_scr)
        consume(nk - 1, sb_scr)
    o_ref[...] = (acc_scr[...] / l_scr[...]).astype(o_ref.dtype)


def _mla_attention(q, k, v, tq, cast_slabs):
    b, h, l, _ = q.shape
    m = k.shape[2]
    nq = l // tq
    steps = b * h * nq
    tk = next(t for t in (1280, 1024, 768, 512, 256, 128) if m % t == 0)
    assert tk % MLA_PV_TILE == 0 and m // tk >= 2

    def slab_spec(w):
        rows = w.shape[0] // steps
        assert rows * steps == w.shape[0] and rows % (2 * V7X_SUBLANES) == 0
        return pl.BlockSpec((rows, w.shape[1]), lambda bi, hi, i: ((bi * h + hi) * nq + i, 0))

    slab_specs = [slab_spec(w) for w in cast_slabs]
    resident = dict(pipeline_mode=pl.Buffered(1))
    out = pl.pallas_call(
        functools.partial(_mla_kernel, tk=tk, nk=m // tk, n_cast=len(cast_slabs)),
        out_shape=[jax.ShapeDtypeStruct((b, l, h * MLA_V), BF16)]
        + [jax.ShapeDtypeStruct(w.shape, BF16) for w in cast_slabs],
        grid=(b, h, nq),
        in_specs=[pl.BlockSpec((None, None, tq, MLA_QK_PAD), lambda bi, hi, i: (bi, hi, i, 0)),
                  pl.BlockSpec((None, None, tq, MLA_QK_PAD),
                               lambda bi, hi, i: (bi, hi, jnp.minimum(i + 1, nq - 1), 0)),
                  pl.BlockSpec((None, None, m, MLA_QK_PAD), lambda bi, hi, i: (bi, hi, 0, 0),
                               **resident),
                  pl.BlockSpec((None, None, m, MLA_V), lambda bi, hi, i: (bi, hi, 0, 0),
                               **resident)] + slab_specs,
        out_specs=[pl.BlockSpec((None, tq, MLA_V), lambda bi, hi, i: (bi, i, hi))] + slab_specs,
        scratch_shapes=[pltpu.VMEM((tq, tk), F32), pltpu.VMEM((tq, tk), F32),
                        pltpu.VMEM((tq, tk), F32),
                        pltpu.VMEM((tq, 1), F32), pltpu.VMEM((tq, 1), F32),
                        pltpu.VMEM((tq, MLA_V), F32)],
        compiler_params=_cparams("arbitrary", "arbitrary", "arbitrary"),
        name="mla_attention",
    )(q, q, k, v, *cast_slabs)
    return out[0], out[1:]


def _na_kernel(q_ref, k_ref, v_ref, kc_ref, vc_ref, gq_ref, gk_ref, bias_ref, o_ref, *, rows):
    rb = pl.program_id(2)
    k_row0 = jnp.clip(rb * NA_Q_ROWS - NA_WIN_ROWS // 2, 0, rows - NA_K_ROWS)
    off = pl.multiple_of(k_row0 * GRID_W, NA_WIN_ROWS // 2 * GRID_W)
    nk = NA_K_ROWS * GRID_W
    q = (_rms(q_ref[...].astype(F32), gq_ref[...]) * (NA_DIM ** -0.5)).astype(BF16)
    kw = _rms(k_ref[pl.ds(off, nk), :].astype(F32), gk_ref[...]).astype(BF16)
    kc = _rms(kc_ref[...].astype(F32), gk_ref[...]).astype(BF16)
    s_w = _dot_nt(q, kw) + bias_ref[...]
    s_c = _dot_nt(q, kc)
    m = jnp.maximum(jnp.max(s_w, axis=-1, keepdims=True), jnp.max(s_c, axis=-1, keepdims=True))
    p_w = jnp.exp(s_w - m)
    p_c = jnp.exp(s_c - m)
    denom = jnp.sum(p_w, axis=-1, keepdims=True) + jnp.sum(p_c, axis=-1, keepdims=True)
    o = (jnp.dot(p_w.astype(BF16), v_ref[pl.ds(off, nk), :], preferred_element_type=F32)
         + jnp.dot(p_c.astype(BF16), vc_ref[...], preferred_element_type=F32))
    o_ref[...] = (o / denom).astype(o_ref.dtype)


def _na_bias_table(rel_bias, rows):
    nrb = rows // NA_Q_ROWS
    tables = []
    for rb in (0, min(1, nrb - 1), nrb - 1):
        k_row0 = int(np.clip(rb * NA_Q_ROWS - NA_WIN_ROWS // 2, 0, rows - NA_K_ROWS))
        r = rb * NA_Q_ROWS + np.arange(NA_Q_ROWS)
        kr = k_row0 + np.arange(NA_K_ROWS)
        qc = np.arange(GRID_W)
        kc = np.arange(GRID_W)
        rs = np.clip(r - NA_WIN_ROWS // 2, 0, rows - NA_WIN_ROWS)
        cs = np.clip(qc - NA_WIN_COLS // 2, 0, GRID_W - NA_WIN_COLS)
        row_ok = (kr[None, :] >= rs[:, None]) & (kr[None, :] < rs[:, None] + NA_WIN_ROWS)
        col_ok = (kc[None, :] >= cs[:, None]) & (kc[None, :] < cs[:, None] + NA_WIN_COLS)
        ri = np.clip(kr[None, :] - r[:, None] + NA_WIN_ROWS - 1, 0, 2 * NA_WIN_ROWS - 2)
        ci = np.clip(kc[None, :] - qc[:, None] + NA_WIN_COLS - 1, 0, 2 * NA_WIN_COLS - 2)
        sel_r = (ri.reshape(-1)[:, None] == np.arange(2 * NA_WIN_ROWS - 1)[None, :]).astype(np.float32)
        sel_c = (np.arange(2 * NA_WIN_COLS - 1)[:, None] == ci.reshape(-1)[None, :]).astype(np.float32)
        bias = jnp.einsum("rm,hmc->hrc", sel_r, rel_bias.astype(F32), precision=lax.Precision.HIGHEST)
        bias = jnp.einsum("hrc,cq->hrq", bias, sel_c, precision=lax.Precision.HIGHEST)
        bias = bias.reshape(NA_HEADS, NA_Q_ROWS, NA_K_ROWS, GRID_W, GRID_W).transpose(0, 1, 3, 2, 4)
        ok = row_ok[:, None, :, None] & col_ok[None, :, None, :]
        bias = jnp.where(ok[None], bias.astype(F32), NEG_BIG)
        tables.append(bias.reshape(NA_HEADS, NA_Q_ROWS * GRID_W, NA_K_ROWS * GRID_W))
    return jnp.stack(tables)


def _na_attention(u3, uctx3, gq, gk, bias):
    b, l, _ = u3.shape
    n_ctx = uctx3.shape[1]
    rows = l // GRID_W
    nrb = rows // NA_Q_ROWS
    tq = NA_Q_ROWS * GRID_W

    def pattern(rb):
        return jnp.where(rb == 0, 0, jnp.where(rb == nrb - 1, 2, 1))

    return pl.pallas_call(
        functools.partial(_na_kernel, rows=rows),
        out_shape=jax.ShapeDtypeStruct((b, l, NA_HEADS * NA_DIM), BF16),
        grid=(b, NA_HEADS, nrb),
        in_specs=[pl.BlockSpec((None, tq, NA_DIM), lambda bi, h, rb: (bi, rb, U0_NAQ_BLK + h)),
                  pl.BlockSpec((None, l, NA_DIM), lambda bi, h, rb: (bi, 0, U0_NAK_BLK + h)),
                  pl.BlockSpec((None, l, NA_DIM), lambda bi, h, rb: (bi, 0, U0_NAV_BLK + h)),
                  pl.BlockSpec((None, n_ctx, NA_DIM), lambda bi, h, rb: (bi, 0, U0_NAK_BLK + h)),
                  pl.BlockSpec((None, n_ctx, NA_DIM), lambda bi, h, rb: (bi, 0, U0_NAV_BLK + h)),
                  pl.BlockSpec((1, NA_DIM), lambda bi, h, rb: (0, 0)),
                  pl.BlockSpec((1, NA_DIM), lambda bi, h, rb: (0, 0)),
                  pl.BlockSpec((None, None, tq, NA_K_ROWS * GRID_W),
                               lambda bi, h, rb: (pattern(rb), h, 0, 0))],
        out_specs=pl.BlockSpec((None, tq, NA_DIM), lambda bi, h, rb: (bi, rb, h)),
        compiler_params=_cparams("parallel", "parallel", "arbitrary"),
        name="neighbourhood_attention",
    )(u3, u3, u3, uctx3, uctx3, gq.reshape(1, -1), gk.reshape(1, -1), bias)


def _attn_out_kernel(am_ref, an_ref, wm_ref, wn_ref, x_ref, gate_ref, o_ref):
    y = (jnp.dot(am_ref[...], wm_ref[...], preferred_element_type=F32)
         + jnp.dot(an_ref[...], wn_ref[...], preferred_element_type=F32))
    o_ref[...] = x_ref[...] + gate_ref[...] * y


def _attn_out(a_m, a_n, w_out_bf16, x2, mod3, gate_blk, tiles_per_batch, tm):
    n, d = x2.shape
    half = a_m.shape[1]
    return pl.pallas_call(
        _attn_out_kernel,
        out_shape=jax.ShapeDtypeStruct((n, d), F32),
        grid=(n // tm,),
        in_specs=[pl.BlockSpec((tm, half), lambda i: (i, 0)),
                  pl.BlockSpec((tm, half), lambda i: (i, 0)),
                  pl.BlockSpec((half, d), lambda i: (0, 0)),
                  pl.BlockSpec((half, d), lambda i: (1, 0)),
                  pl.BlockSpec((tm, d), lambda i: (i, 0)),
                  pl.BlockSpec((None, 1, d), lambda i: (i // tiles_per_batch, 0, gate_blk))],
        out_specs=pl.BlockSpec((tm, d), lambda i: (i, 0)),
        compiler_params=_cparams("parallel"),
        name="attn_out_residual",
    )(a_m, a_n, w_out_bf16, w_out_bf16, x2, mod3)


def _pool_conv_kernel(prev_ref, u_ref, next_ref, pw_ref, ps_ref, cw_ref, wp_ref, wc_ref, x_ref,
                      gate_ref, o_ref, *, seq_len, tiles_per_batch):
    tm = u_ref.shape[0]
    n_ext = tm + 2 * HALO
    pos0 = (pl.program_id(0) % tiles_per_batch) * tm
    pos_ext = pos0 - HALO + lax.broadcasted_iota(jnp.int32, (n_ext, 1), 0)
    valid = (pos_ext >= 0) & (pos_ext < seq_len)

    def ext(c0, c1):
        e = jnp.concatenate([prev_ref[:, c0:c1], u_ref[:, c0:c1], next_ref[:, c0:c1]], axis=0)
        return jnp.where(valid, e.astype(F32), 0.0)

    def shifted(a, d):
        return pltpu.roll(a, (-d) % n_ext, axis=0)

    pos = (pos0 + lax.broadcasted_iota(jnp.int32, (tm, 1), 0)).astype(F32)
    y = jnp.zeros((tm, x_ref.shape[1]), F32)
    for g, w in enumerate(POOL_WINDOWS):
        c0 = g * POOL_GROUP
        e = ext(c0, c0 + POOL_GROUP)
        acc = e + shifted(e, -1)
        span = 1
        while 2 * span < w:
            acc = shifted(acc, span) + shifted(acc, -span)
            span *= 2
        half = w // 2
        cnt = jnp.minimum(pos + half, float(seq_len)) - jnp.maximum(pos - half, 0.0)
        pooled = acc[HALO:HALO + tm] / cnt - e[HALO:HALO + tm]
        mixed = jnp.dot(pooled.astype(BF16), pw_ref[g], preferred_element_type=F32)
        y_g = (mixed * ps_ref[:, c0:c0 + POOL_GROUP]).astype(BF16)
        y = y + jnp.dot(y_g, wp_ref[c0:c0 + POOL_GROUP, :], preferred_element_type=F32)

    gb0 = POOL_WIDTH
    gc0 = POOL_WIDTH + CONV_CH
    v0 = POOL_WIDTH + 2 * CONV_CH
    z = ext(gc0, gc0 + CONV_CH) * ext(v0, v0 + CONV_CH)
    cw = cw_ref[...]
    conv = cw[0:1] * shifted(z, -1) + cw[1:2] * z + cw[2:3] * shifted(z, 1)
    y_conv = u_ref[:, gb0:gb0 + CONV_CH].astype(F32) * conv[HALO:HALO + tm]
    y = y + jnp.dot(y_conv.astype(BF16), wc_ref[...], preferred_element_type=F32)
    o_ref[...] = x_ref[...] + gate_ref[...] * y


def _pool_conv_mixer(u, pool_w, pool_scale, conv_w, w_out_bf16, x2, mod3, gate_blk, seq_len, tm):
    n, d = x2.shape
    width = u.shape[1]
    tiles_per_batch = seq_len // tm
    hb = tm // HALO
    last = n // HALO - 1
    return pl.pallas_call(
        functools.partial(_pool_conv_kernel, seq_len=seq_len, tiles_per_batch=tiles_per_batch),
        out_shape=jax.ShapeDtypeStruct((n, d), F32),
        grid=(n // tm,),
        in_specs=[pl.BlockSpec((HALO, width), lambda i: (jnp.maximum(i * hb - 1, 0), 0)),
                  pl.BlockSpec((tm, width), lambda i: (i, 0)),
                  pl.BlockSpec((HALO, width), lambda i: (jnp.minimum((i + 1) * hb, last), 0)),
                  pl.BlockSpec(pool_w.shape, lambda i: (0, 0, 0)),
                  pl.BlockSpec((1, POOL_WIDTH), lambda i: (0, 0)),
                  pl.BlockSpec((V7X_SUBLANES, CONV_CH), lambda i: (0, 0)),
                  pl.BlockSpec((POOL_WIDTH, d), lambda i: (0, 0)),
                  pl.BlockSpec((CONV_CH, d), lambda i: (1, 0)),
                  pl.BlockSpec((tm, d), lambda i: (i, 0)),
                  pl.BlockSpec((None, 1, d), lambda i: (i // tiles_per_batch, 0, gate_blk))],
        out_specs=pl.BlockSpec((tm, d), lambda i: (i, 0)),
        compiler_params=_cparams("parallel"),
        name="pool_conv_mixer",
    )(u, u, u, pool_w.astype(BF16), pool_scale.reshape(1, -1),
      jnp.pad(conv_w, ((0, V7X_SUBLANES - conv_w.shape[0]), (0, 0))), w_out_bf16, w_out_bf16, x2, mod3)


def _split_bf16(a):
    hi = a.astype(BF16)
    return hi, (a - hi.astype(F32)).astype(BF16)


def _pack_bf16_pairs(lo, hi):
    lo_bits = lax.bitcast_convert_type(lo.astype(BF16).astype(F32), jnp.uint32) >> 16
    hi_bits = lax.bitcast_convert_type(hi.astype(BF16).astype(F32), jnp.uint32) & jnp.uint32(0xFFFF0000)
    return hi_bits | lo_bits


def _unpack_pairs_f32(w):
    lo = lax.bitcast_convert_type(w << 16, F32)
    hi = lax.bitcast_convert_type(w & jnp.uint32(0xFFFF0000), F32)
    return lo, hi


def _unpack_bf16_pairs(w):
    lo, hi = _unpack_pairs_f32(w)
    return lo.astype(BF16), hi.astype(BF16)


def _router_kernel(x_ref, g_ref, sh_ref, sc_ref, wr_ref, br_ref, tri_ref, hp_ref, idx_ref, gate_ref,
                   rank_ref, cnt_ref, run_scr, *, p_rows):
    tm, d = x_ref.shape

    @pl.when(pl.program_id(0) == 0)
    def _():
        run_scr[...] = jnp.zeros_like(run_scr)

    h = _rms(x_ref[...], g_ref[...]) * (1.0 + sc_ref[...]) + sh_ref[...]
    packed = _pack_bf16_pairs(h[:, :d // 2], h[:, d // 2:])
    for s in range(p_rows):
        hp_ref[pl.ds(s, tm, stride=p_rows), :] = packed[:, s * V7X_LANES:(s + 1) * V7X_LANES]
    h_hi, h_lo = _split_bf16(h)
    w_hi, w_lo = _split_bf16(wr_ref[...])
    logits = _dot_nt(w_hi, h_hi) + _dot_nt(w_hi, h_lo) + _dot_nt(w_lo, h_hi) + br_ref[...]
    expert = lax.broadcasted_iota(jnp.int32, logits.shape, 0)
    vals, ids = [], []
    for _ in range(TOP_K):
        mx = jnp.max(logits, axis=0, keepdims=True)
        am = jnp.min(jnp.where(logits == mx, expert, N_EXPERTS), axis=0, keepdims=True)
        vals.append(mx)
        ids.append(am)
        logits = jnp.where(expert == am, -jnp.inf, logits)
    ex = [jnp.exp(v - vals[0]) for v in vals]
    tot = ex[0] + ex[1] + ex[2] + ex[3]
    idx_ref[...] = jnp.concatenate(ids, axis=0)
    gate_ref[...] = jnp.concatenate([e / tot for e in ex], axis=0)

    running = run_scr[...]
    ranks = []
    for k in range(TOP_K):
        hit = expert == ids[k]
        onehot = jnp.where(hit, 1.0, 0.0)
        before = jnp.dot(onehot.astype(BF16), tri_ref[...], preferred_element_type=F32)
        rank = jnp.sum(jnp.where(hit, before + running, 0.0), axis=0, keepdims=True)
        ranks.append(rank.astype(jnp.int32))
        running = running + jnp.sum(onehot, axis=1, keepdims=True)
    run_scr[...] = running
    rank_ref[...] = jnp.concatenate(ranks, axis=0)
    cnt_ref[...] = jnp.broadcast_to(running, cnt_ref.shape).astype(jnp.int32)


def _router(x2, gain, mod3, shift_blk, scale_blk, tiles_per_batch, w_router, b_router, tm):
    n, d = x2.shape
    p_rows = d // 2 // V7X_LANES
    slot = jax.ShapeDtypeStruct((TOP_K, n), jnp.int32)
    slot_spec = pl.BlockSpec((TOP_K, tm), lambda i: (0, i))
    tri = jnp.triu(jnp.ones((tm, tm), BF16), k=1)
    return pl.pallas_call(
        functools.partial(_router_kernel, p_rows=p_rows),
        out_shape=(jax.ShapeDtypeStruct((n * p_rows, V7X_LANES), jnp.uint32),
                   slot, jax.ShapeDtypeStruct((TOP_K, n), F32), slot,
                   jax.ShapeDtypeStruct((N_EXPERTS, V7X_LANES), jnp.int32)),
        grid=(n // tm,),
        in_specs=[pl.BlockSpec((tm, d), lambda i: (i, 0)),
                  pl.BlockSpec((1, d), lambda i: (0, 0)),
                  pl.BlockSpec((None, 1, d), lambda i: (i // tiles_per_batch, 0, shift_blk)),
                  pl.BlockSpec((None, 1, d), lambda i: (i // tiles_per_batch, 0, scale_blk)),
                  pl.BlockSpec((N_EXPERTS, d), lambda i: (0, 0)),
                  pl.BlockSpec((N_EXPERTS, 1), lambda i: (0, 0)),
                  pl.BlockSpec((tm, tm), lambda i: (0, 0))],
        out_specs=(pl.BlockSpec((tm * p_rows, V7X_LANES), lambda i: (i, 0)),
                   slot_spec, slot_spec, slot_spec,
                   pl.BlockSpec((N_EXPERTS, V7X_LANES), lambda i: (0, 0))),
        scratch_shapes=[pltpu.VMEM((N_EXPERTS, 1), F32)],
        compiler_params=_cparams("arbitrary"),
        name="moe_router",
    )(x2, gain.reshape(1, d), mod3, mod3, w_router.T, b_router.reshape(-1, 1), tri)


def _slot_pos_kernel(pstart_ref, idx_ref, rank_ref, pos_ref):
    idx = idx_ref[...]
    pos = rank_ref[...]
    for e in range(N_EXPERTS):
        pos = pos + jnp.where(idx == e, pstart_ref[e], 0)
    pos_ref[...] = pos


def _slot_positions(pstart, idx_t, rank_t):
    k, n = idx_t.shape
    tn = next(t for t in (4096, 2048, 1024, 512) if n % t == 0)
    spec = pl.BlockSpec((k, tn), lambda i, ps: (0, i))
    return pl.pallas_call(
        _slot_pos_kernel,
        out_shape=jax.ShapeDtypeStruct((k, n), jnp.int32),
        grid_spec=pltpu.PrefetchScalarGridSpec(
            num_scalar_prefetch=1, grid=(n // tn,), in_specs=[spec, spec], out_specs=spec),
        compiler_params=_cparams("parallel"),
        name="moe_slot_positions",
    )(pstart, idx_t, rank_t)


def _start_row_dmas(idx_hbm_row, idx_smem, idx_sem, row_copy):
    cp = pltpu.make_async_copy(idx_hbm_row, idx_smem, idx_sem)
    cp.start()
    cp.wait()

    def issue(pair, c):
        r = 2 * pair
        row_copy(r, idx_smem[r]).start(priority=0)
        row_copy(r + 1, idx_smem[r + 1]).start(priority=1)
        return c

    lax.fori_loop(0, idx_smem.shape[0] // 2, issue, 0, unroll=4)


def _wait_row_dmas(n, row_copy):
    def drain(r, c):
        row_copy(r, 0).wait()
        return c

    lax.fori_loop(0, n, drain, 0, unroll=8)


def _dispatch_kernel(pos_hbm, hp_ref, xs_zero_hbm, xs_hbm, idx_smem, sem, idx_sem, *, p_rows):
    del xs_zero_hbm
    tt = hp_ref.shape[0] // p_rows

    def row_copy(r, dst_row):
        t = r & (tt - 1)
        src = hp_ref.at[pl.ds(pl.multiple_of(t * p_rows, p_rows), p_rows)]
        return pltpu.make_async_copy(src, xs_hbm.at[dst_row], sem)

    _start_row_dmas(pos_hbm.at[pl.program_id(0)], idx_smem, idx_sem, row_copy)
    _wait_row_dmas(idx_smem.shape[0], row_copy)


def _dispatch(pos_tiles, h_packed, n_rows, p_rows, row_buffer):
    n_tiles, slots = pos_tiles.shape
    tt = slots // TOP_K
    assert tt & (tt - 1) == 0
    xs_shape = jax.ShapeDtypeStruct((n_rows, p_rows, V7X_LANES), jnp.uint32)
    if row_buffer is None:
        row_buffer = jnp.zeros(xs_shape.shape, xs_shape.dtype)
    assert row_buffer.shape == xs_shape.shape and row_buffer.dtype == xs_shape.dtype
    return pl.pallas_call(
        functools.partial(_dispatch_kernel, p_rows=p_rows),
        out_shape=xs_shape,
        grid=(n_tiles,),
        in_specs=[pl.BlockSpec(memory_space=pl.ANY),
                  pl.BlockSpec((tt * p_rows, V7X_LANES), lambda i: (i, 0)),
                  pl.BlockSpec(memory_space=pl.ANY)],
        out_specs=pl.BlockSpec(memory_space=pl.ANY),
        scratch_shapes=[pltpu.SMEM((slots,), jnp.int32),
                        pltpu.SemaphoreType.DMA,
                        pltpu.SemaphoreType.DMA],
        input_output_aliases={2: 0},
        compiler_params=_cparams("arbitrary"),
        name="moe_dispatch",
    )(pos_tiles, h_packed, row_buffer)


def _experts_kernel(be_ref, nused_ref, xs_ref, wg_ref, wu_ref, bg_ref, bu_ref, wd_ref, bd_ref,
                    y_ref, x_scr, acc_scr, *, p_rows):
    i = pl.program_id(0)
    j = pl.program_id(1)
    nf = pl.num_programs(1)
    tb, d = x_scr.shape
    active = i < nused_ref[0]

    def hidden_chunk(x):
        a_gate = jnp.dot(x, wg_ref[...], preferred_element_type=F32) + bg_ref[...]
        a_up = jnp.dot(x, wu_ref[...], preferred_element_type=F32) + bu_ref[...]
        a_gate = jnp.minimum(a_gate, SWIGLU_LIMIT)
        a_up = jnp.clip(a_up, -SWIGLU_LIMIT, SWIGLU_LIMIT)
        act = (a_up + 1.0) * (a_gate * jax.nn.sigmoid(SWIGLU_ALPHA * a_gate))
        return jnp.dot(act.astype(BF16), wd_ref[...], preferred_element_type=F32)

    def store_rows(y):
        packed = _pack_bf16_pairs(y[:, :d // 2], y[:, d // 2:])
        for s in range(p_rows):
            y_ref[pl.ds(s, tb, stride=p_rows), :] = packed[:, s * V7X_LANES:(s + 1) * V7X_LANES]

    @pl.when(active & (j == 0))
    def _():
        for s in range(p_rows):
            lo, hi = _unpack_bf16_pairs(xs_ref[pl.ds(s, tb, stride=p_rows), :])
            x_scr[:, s * V7X_LANES:(s + 1) * V7X_LANES] = lo
            x_scr[:, d // 2 + s * V7X_LANES:d // 2 + (s + 1) * V7X_LANES] = hi
        acc_scr[...] = hidden_chunk(x_scr[...])

    @pl.when(active & (j > 0) & (j < nf - 1))
    def _():
        acc_scr[...] += hidden_chunk(x_scr[...])

    @pl.when(active & (j > 0) & (j == nf - 1))
    def _():
        store_rows(acc_scr[...] + hidden_chunk(x_scr[...]) + bd_ref[...])

    @pl.when(jnp.logical_not(active) & (j == nf - 1))
    def _():
        y_ref[...] = jnp.zeros_like(y_ref)


def _experts(block_expert, n_used, xs_rows, tb, w_gate_up, b_gate_up, w_down, b_down):
    e, d, two_f = w_gate_up.shape
    p_rows = d // 2 // V7X_LANES
    n_blocks = xs_rows.shape[0] // (tb * p_rows)
    f = two_f // 2
    tf = MOE_F_TILE
    nf = f // tf
    assert nf >= 2 and nf * tf == f
    row_spec = pl.BlockSpec((tb * p_rows, V7X_LANES), lambda i, j, be, nu: (i, 0))

    def fcol(i, j, nused):
        return jnp.where(i < nused[0], j, nf - 1)

    grid_spec = pltpu.PrefetchScalarGridSpec(
        num_scalar_prefetch=2,
        grid=(n_blocks, nf),
        in_specs=[row_spec,
                  pl.BlockSpec((None, d, tf), lambda i, j, be, nu: (be[i], 0, fcol(i, j, nu))),
                  pl.BlockSpec((None, d, tf), lambda i, j, be, nu: (be[i], 0, nf + fcol(i, j, nu))),
                  pl.BlockSpec((None, 1, tf), lambda i, j, be, nu: (be[i], 0, fcol(i, j, nu))),
                  pl.BlockSpec((None, 1, tf), lambda i, j, be, nu: (be[i], 0, nf + fcol(i, j, nu))),
                  pl.BlockSpec((None, tf, d), lambda i, j, be, nu: (be[i], fcol(i, j, nu), 0)),
                  pl.BlockSpec((None, 1, d), lambda i, j, be, nu: (be[i], 0, 0))],
        out_specs=row_spec,
        scratch_shapes=[pltpu.VMEM((tb, d), BF16),
                        pltpu.VMEM((tb, d), F32)])
    return pl.pallas_call(
        functools.partial(_experts_kernel, p_rows=p_rows),
        out_shape=jax.ShapeDtypeStruct(xs_rows.shape, jnp.uint32),
        grid_spec=grid_spec,
        compiler_params=_cparams("arbitrary", "arbitrary"),
        name="moe_experts",
    )(block_expert, n_used, xs_rows, w_gate_up, w_gate_up,
      b_gate_up.reshape(e, 1, two_f), b_gate_up.reshape(e, 1, two_f), w_down,
      b_down.reshape(e, 1, d))


def _combine_kernel(pos_hbm, y_hbm, x_ref, mgate_ref, rgate_ref, o_ref, idx_smem, gbuf_a, gbuf_b,
                    sem_a, sem_b, idx_sem, *, p_rows):
    i = pl.program_id(0)
    n_steps = pl.num_programs(0)
    tt, d = x_ref.shape
    n_rows = TOP_K * tt

    def row_copier(gbuf, sem):
        def row_copy(r, src_row):
            dst = gbuf.at[pl.ds(pl.multiple_of(r * p_rows, p_rows), p_rows)]
            return pltpu.make_async_copy(y_hbm.at[src_row], dst, sem)
        return row_copy

    def step(cur, nxt, first):
        if first:
            @pl.when(i == 0)
            def _():
                _start_row_dmas(pos_hbm.at[0], idx_smem, idx_sem, row_copier(*cur))

        @pl.when(i + 1 < n_steps)
        def _():
            _start_row_dmas(pos_hbm.at[i + 1], idx_smem, idx_sem, row_copier(*nxt))

        _wait_row_dmas(n_rows, row_copier(*cur))
        gbuf = cur[0]
        rg = rgate_ref[...]
        gates = [jnp.broadcast_to(rg[:, k:k + 1], (tt, V7X_LANES)) for k in range(TOP_K)]
        for s in range(p_rows):
            lo_tot = jnp.zeros((tt, V7X_LANES), F32)
            hi_tot = jnp.zeros((tt, V7X_LANES), F32)
            for k in range(TOP_K):
                lo, hi = _unpack_pairs_f32(gbuf[pl.ds(k * tt * p_rows + s, tt, stride=p_rows), :])
                lo_tot = lo_tot + gates[k] * lo
                hi_tot = hi_tot + gates[k] * hi
            for base, tot in ((0, lo_tot), (d // 2, hi_tot)):
                cols = slice(base + s * V7X_LANES, base + (s + 1) * V7X_LANES)
                o_ref[:, cols] = x_ref[:, cols] + mgate_ref[:, cols] * tot

    @pl.when(i % 2 == 0)
    def _():
        step((gbuf_a, sem_a), (gbuf_b, sem_b), True)

    @pl.when(i % 2 == 1)
    def _():
        step((gbuf_b, sem_b), (gbuf_a, sem_a), False)


def _combine(pos_tiles, y_rows, x2, mod3, gate_blk, router_gates, tiles_per_batch, tt):
    n, d = x2.shape
    p_rows = y_rows.shape[1]
    return pl.pallas_call(
        functools.partial(_combine_kernel, p_rows=p_rows),
        out_shape=jax.ShapeDtypeStruct((n, d), F32),
        grid=(n // tt,),
        in_specs=[pl.BlockSpec(memory_space=pl.ANY),
                  pl.BlockSpec(memory_space=pl.ANY),
                  pl.BlockSpec((tt, d), lambda i: (i, 0)),
                  pl.BlockSpec((None, 1, d), lambda i: (i // tiles_per_batch, 0, gate_blk)),
                  pl.BlockSpec((tt, TOP_K), lambda i: (i, 0))],
        out_specs=pl.BlockSpec((tt, d), lambda i: (i, 0)),
        scratch_shapes=[pltpu.SMEM((TOP_K * tt,), jnp.int32),
                        pltpu.VMEM((TOP_K * tt * p_rows, V7X_LANES), jnp.uint32),
                        pltpu.VMEM((TOP_K * tt * p_rows, V7X_LANES), jnp.uint32),
                        pltpu.SemaphoreType.DMA,
                        pltpu.SemaphoreType.DMA,
                        pltpu.SemaphoreType.DMA],
        compiler_params=_cparams("arbitrary"),
        name="moe_combine",
    )(pos_tiles, y_rows, x2, mod3, router_gates)


def _block_tables(counts, n_slots, tb):
    padded = (counts + tb - 1) // tb * tb
    pend = jnp.cumsum(padded)
    pstart = (pend - padded).astype(jnp.int32)
    n_blocks = -(-n_slots // tb) + N_EXPERTS
    block_start = jnp.arange(n_blocks, dtype=jnp.int32) * tb
    block_expert = jnp.minimum(
        jnp.sum(block_start[:, None] >= pend[None, :], axis=1), N_EXPERTS - 1).astype(jnp.int32)
    n_used = (pend[-1] // tb).astype(jnp.int32).reshape(1)
    return pstart, block_expert, n_used, n_blocks


def _moe(x2, norm2, mod3, tiles_per_batch_fn, w_router, b_router, w_gate_up, b_gate_up, w_down,
         b_down, row_buffer=None):
    n, d = x2.shape
    p_rows = d // 2 // V7X_LANES
    h_packed, idx_t, gate_t, rank_t, counts = _router(
        x2, norm2, mod3, 3, 4, tiles_per_batch_fn(ROW_TILE), w_router, b_router, ROW_TILE)
    pstart, block_expert, n_used, n_blocks = _block_tables(counts[:, 0], TOP_K * n, MOE_ROWS)
    pos = _slot_positions(pstart, idx_t, rank_t)
    tt = COMBINE_TOKENS
    pos_tiles = pos.reshape(TOP_K, n // tt, tt).transpose(1, 0, 2).reshape(n // tt, TOP_K * tt)
    xs = _dispatch(pos_tiles, h_packed, n_blocks * MOE_ROWS, p_rows, row_buffer)
    y_rows = _experts(block_expert, n_used, xs.reshape(-1, V7X_LANES), MOE_ROWS,
                      w_gate_up.astype(BF16), b_gate_up, w_down.astype(BF16), b_down)
    out = _combine(pos_tiles, y_rows.reshape(-1, p_rows, V7X_LANES), x2, mod3, 5, gate_t.T,
                   tiles_per_batch_fn(tt), tt)
    return out, xs


def _rope_swap_perm():
    half = MLA_ROPE // 2
    quarter = half // 2
    j = np.arange(MLA_ROPE)
    return (j // half) * half + (j % half + quarter) % half


def _rope_tables(n_tokens):
    t = jnp.arange(n_tokens, dtype=jnp.int32)
    row = (t // GRID_W).astype(F32)
    col = (t % GRID_W).astype(F32)
    n_freq = MLA_ROPE // 4
    inv_freq = ROPE_BASE ** (-jnp.arange(n_freq, dtype=F32) / n_freq)
    ar = row[:, None] * inv_freq
    ac = col[:, None] * inv_freq
    cos_t = jnp.concatenate([jnp.cos(ar), jnp.cos(ar), jnp.cos(ac), jnp.cos(ac)], axis=-1)
    sin_t = jnp.concatenate([-jnp.sin(ar), jnp.sin(ar), -jnp.sin(ac), jnp.sin(ac)], axis=-1)
    return cos_t, sin_t


def kernel(x, c, ctx, c_ctx, l0_w_mod, l0_b_mod, l0_norm1, l0_w_in, l0_mla_q_norm, l0_mla_w_q_up, l0_mla_kv_norm, l0_mla_w_kv_up, l0_mla_qk_q, l0_mla_qk_k, l0_na_qk_q, l0_na_qk_k, l0_na_rel_bias, l0_w_out, l0_norm2, l0_w_router, l0_b_router, l0_w_gate_up, l0_b_gate_up, l0_w_down, l0_b_down, l1_w_mod, l1_b_mod, l1_norm1, l1_w_in, l1_pool_w, l1_pool_scale, l1_conv_w, l1_w_out, l1_norm2, l1_w_router, l1_b_router, l1_w_gate_up, l1_b_gate_up, l1_w_down, l1_b_down):
    b, l, d = x.shape
    n_ctx = ctx.shape[1]
    n = b * l
    assert d % V7X_LANES == 0 and l % ROW_TILE == 0 and l % (NA_Q_ROWS * GRID_W) == 0
    assert l // GRID_W >= NA_K_ROWS and b + 1 <= V7X_SUBLANES
    x2 = x.reshape(n, d)

    def tiles_per_batch(tm):
        return l // tm

    c_rows = jnp.concatenate(
        [c, c_ctx[None], jnp.zeros((V7X_SUBLANES - b - 1, d), F32)], axis=0)
    perm = _rope_swap_perm()

    mod3 = _modulation(c_rows, l0_w_mod, l0_b_mod).reshape(V7X_SUBLANES, 1, 6 * d)
    kr0 = MLA_Q_RANK + MLA_KV_RANK
    kr1 = kr0 + MLA_ROPE
    w_in = jnp.concatenate(
        [l0_w_in[:, :kr1], l0_w_in[:, kr0:kr1][:, perm], l0_w_in[:, kr1:],
         jnp.zeros((d, U0_COLS - l0_w_in.shape[1] - MLA_ROPE), F32)], axis=1).astype(BF16)
    tpb = tiles_per_batch(ROW_TILE)
    u = _norm_proj(x2, l0_norm1, mod3, 0, 1, lambda i: i // tpb, w_in, ROW_TILE)
    ctx_tile = n_ctx if (b * n_ctx) % ROW_TILE else ROW_TILE
    u_ctx = _norm_proj(ctx.reshape(b * n_ctx, d), l0_norm1, mod3, 0, 1, lambda i: b, w_in, ctx_tile)

    wq = l0_mla_w_q_up.reshape(MLA_Q_RANK, MLA_HEADS, MLA_QK)
    wq = jnp.concatenate([wq, wq[:, :, MLA_NOPE:][:, :, perm]], axis=-1)
    wq = wq.reshape(MLA_Q_RANK, -1).astype(BF16)
    gq = jnp.concatenate([l0_mla_qk_q, l0_mla_qk_q[MLA_NOPE:][perm]]).reshape(1, -1)
    gk = jnp.concatenate([l0_mla_qk_k, l0_mla_qk_k[MLA_NOPE:][perm]]).reshape(1, -1)
    wkv = l0_mla_w_kv_up.astype(BF16)
    cos_t, sin_t = _rope_tables(l)
    cos_k = jnp.concatenate([cos_t, jnp.ones((n_ctx, MLA_ROPE), F32)], axis=0)
    sin_k = jnp.concatenate([sin_t, jnp.zeros((n_ctx, MLA_ROPE), F32)], axis=0)
    q = _q_prep(u, b, l, l0_mla_q_norm, wq, gq, cos_t, sin_t, ROW_TILE)
    u3 = u.reshape(b, l, U0_COLS)
    uctx3 = u_ctx.reshape(b, n_ctx, U0_COLS)
    ckv0 = U0_CKV_BLK * V7X_LANES
    ckv1 = ckv0 + MLA_KV_RANK + 2 * MLA_ROPE
    ckv = jnp.concatenate([u3[:, :, ckv0:ckv1], uctx3[:, :, ckv0:ckv1]], axis=1)
    k_m, v_m = _kv_prep(ckv, l0_mla_kv_norm, wkv, gk, cos_k, sin_k, n_ctx)
    expert_w = (l0_w_gate_up, l0_w_down, l1_w_gate_up, l1_w_down)
    o_m, expert_w_bf16 = _mla_attention(
        q, k_m, v_m, ROW_TILE, [w.reshape(-1, w.shape[-1]) for w in expert_w])
    l0_w_gate_up, l0_w_down, l1_w_gate_up, l1_w_down = (
        wb.reshape(w.shape) for wb, w in zip(expert_w_bf16, expert_w))

    bias = _na_bias_table(l0_na_rel_bias, l // GRID_W)
    o_n = _na_attention(u3, uctx3, l0_na_qk_q, l0_na_qk_k, bias)

    x2 = _attn_out(o_m.reshape(n, -1), o_n.reshape(n, -1), l0_w_out.astype(BF16), x2, mod3, 2,
                   tpb, ROW_TILE)
    x2, expert_rows = _moe(x2, l0_norm2, mod3, tiles_per_batch, l0_w_router, l0_b_router,
                           l0_w_gate_up, l0_b_gate_up, l0_w_down, l0_b_down)

    mod3 = _modulation(c_rows, l1_w_mod, l1_b_mod).reshape(V7X_SUBLANES, 1, 6 * d)
    u = _norm_proj(x2, l1_norm1, mod3, 0, 1, lambda i: i // tpb, l1_w_in.astype(BF16), ROW_TILE)
    x2 = _pool_conv_mixer(u, l1_pool_w, l1_pool_scale, l1_conv_w, l1_w_out.astype(BF16), x2, mod3,
                          2, l, ROW_TILE)
    x2, _ = _moe(x2, l1_norm2, mod3, tiles_per_batch, l1_w_router, l1_b_router, l1_w_gate_up,
                 l1_b_gate_up, l1_w_down, l1_b_down, row_buffer=expert_rows)
    return x2.reshape(b, l, d)
```
